```python
import math
import jax
import jax.numpy as jnp
from jax import lax
import numpy as np

D_MODEL = 2048
BATCH = 4
SEQ = 4096
DEPTH = 4
DEC_BATCH = 8
DEC_SEQ = 4096
PAST_LEN = 128

N_MIXERS = 2
HEAD_DIM = 128
A_Q_HEADS = D_MODEL // HEAD_DIM
A_KV_HEADS = 4
A_GROUP = A_Q_HEADS // A_KV_HEADS
WINDOW = 128
BLOCK = 128
KPAD = ((WINDOW + BLOCK - 1) // BLOCK) * BLOCK
B_HEADS = D_MODEL // HEAD_DIM
B_QK_DIM = HEAD_DIM // 2
B_V_DIM = HEAD_DIM
Q_BLOCK = 128
ROPE_THETA = 500000.0
ROPE_FRACTION = 4
N_EXPERTS = 16
N_GROUPS = 4
EXPERTS_PER_GROUP = N_EXPERTS // N_GROUPS
TOP_K = 2
D_EXPERT = 1408
MOE_BLOCK = 256
PL_DIM = 256
ALPHA = (2.0 * DEPTH) ** 0.25
BETA = (8.0 * DEPTH) ** -0.25
LN_EPS = 1e-5
NEG_INF = -1e30

kernel_name = "hybrid_swa_diffattn_groupmoe_encoder"


def rope_partial(x, pos):
    rd = x.shape[-1] // ROPE_FRACTION
    half = rd // 2
    inv_freq = ROPE_THETA ** (-jnp.arange(half, dtype=jnp.float32) / half)
    ang = pos.astype(jnp.float32)[:, None] * inv_freq[None, :]
    cos = jnp.cos(ang)[:, None, :].astype(x.dtype)
    sin = jnp.sin(ang)[:, None, :].astype(x.dtype)
    x1, x2, rest = x[..., :half], x[..., half:rd], x[..., rd:]
    return jnp.concatenate([x1 * cos - x2 * sin, x2 * cos + x1 * sin, rest], axis=-1)


def layer_norm(x, g, b):
    xf = x.astype(jnp.float32)
    xc = xf - jnp.mean(xf, -1, keepdims=True)
    var = jnp.mean(xc * xc, -1, keepdims=True)
    y = xc * lax.rsqrt(var + LN_EPS) * g.astype(jnp.float32) + b.astype(jnp.float32)
    return y.astype(x.dtype)


def rms_norm(x, g):
    xf = x.astype(jnp.float32)
    y = xf * lax.rsqrt(jnp.mean(xf * xf, -1, keepdims=True) + LN_EPS) * g.astype(jnp.float32)
    return y.astype(x.dtype)


def window_gqa_sink(x, w_qkv, w_o, sink, pos):
    bsz, seq, _ = x.shape
    qkv = x @ w_qkv
    dq = A_Q_HEADS * HEAD_DIM
    dk = A_KV_HEADS * HEAD_DIM
    q = rope_partial(qkv[..., :dq].reshape(bsz, seq, A_Q_HEADS, HEAD_DIM), pos)
    q = q.reshape(bsz, seq, A_KV_HEADS, A_GROUP, HEAD_DIM)
    k = rope_partial(qkv[..., dq:dq + dk].reshape(bsz, seq, A_KV_HEADS, HEAD_DIM), pos)
    v = qkv[..., dq + dk:].reshape(bsz, seq, A_KV_HEADS, HEAD_DIM)
    pad = ((0, 0), (KPAD, KPAD), (0, 0), (0, 0))
    kp = jnp.pad(k, pad)
    vp = jnp.pad(v, pad)
    sink_f = sink.astype(jnp.float32).reshape(A_KV_HEADS, A_GROUP)[None, :, :, None, None]
    scale = HEAD_DIM ** -0.5
    span = BLOCK + 2 * KPAD

    def block(n):
        start = n * BLOCK
        qb = lax.dynamic_slice_in_dim(q, start, BLOCK, axis=1)
        kb = lax.dynamic_slice_in_dim(kp, start, span, axis=1)
        vb = lax.dynamic_slice_in_dim(vp, start, span, axis=1)
        s = jnp.einsum('bqhgd,bkhd->bhgqk', qb, kb).astype(jnp.float32) * scale
        qpos = start + jnp.arange(BLOCK)
        kpos = start - KPAD + jnp.arange(span)
        valid = ((jnp.abs(qpos[:, None] - kpos[None, :]) <= WINDOW)
                 & (kpos[None, :] >= 0) & (kpos[None, :] < seq))
        s = jnp.where(valid, s, NEG_INF)
        m = jnp.maximum(jnp.max(s, -1, keepdims=True), sink_f)
        e = jnp.exp(s - m)
        den = jnp.sum(e, -1, keepdims=True) + jnp.exp(sink_f - m)
        pr = (e / den).astype(v.dtype)
        return jnp.einsum('bhgqk,bkhd->bqhgd', pr, vb)

    o = lax.map(block, jnp.arange(seq // BLOCK))
    o = jnp.moveaxis(o, 0, 1).reshape(bsz, seq, A_Q_HEADS * HEAD_DIM)
    return o @ w_o


def diff_attention(x, w_qkv, w_o, lq1, lk1, lq2, lk2, subln_g, lambda_init, pos):
    bsz, seq, _ = x.shape
    qkv = x @ w_qkv
    dqk = B_HEADS * 2 * B_QK_DIM
    q = rope_partial(qkv[..., :dqk].reshape(bsz, seq, 2 * B_HEADS, B_QK_DIM), pos)
    q = q.reshape(bsz, seq, B_HEADS, 2, B_QK_DIM)
    k = rope_partial(qkv[..., dqk:2 * dqk].reshape(bsz, seq, 2 * B_HEADS, B_QK_DIM), pos)
    k = k.reshape(bsz, seq, B_HEADS, 2, B_QK_DIM)
    v = qkv[..., 2 * dqk:].reshape(bsz, seq, B_HEADS, B_V_DIM)
    f32 = jnp.float32
    lam = (jnp.exp(jnp.sum(lq1.astype(f32) * lk1.astype(f32)))
           - jnp.exp(jnp.sum(lq2.astype(f32) * lk2.astype(f32))) + lambda_init)
    scale = B_QK_DIM ** -0.5

    def block(n):
        qb = lax.dynamic_slice_in_dim(q, n * Q_BLOCK, Q_BLOCK, axis=1)
        s = jnp.einsum('bqhcd,bkhcd->bchqk', qb, k).astype(f32) * scale
        pr = jax.nn.softmax(s, axis=-1)
        a = (pr[:, 0] - lam * pr[:, 1]).astype(v.dtype)
        return jnp.einsum('bhqk,bkhd->bqhd', a, v)

    o = lax.map(block, jnp.arange(seq // Q_BLOCK))
    o = jnp.moveaxis(o, 0, 1).reshape(bsz, seq, B_HEADS, B_V_DIM)
    o = rms_norm(o, subln_g) * (1.0 - lambda_init)
    return o.reshape(bsz, seq, B_HEADS * B_V_DIM) @ w_o


def group_limited_route(x2d, router_w, router_bias):
    t = x2d.shape[0]
    s = jax.nn.sigmoid((x2d @ router_w).astype(jnp.float32))
    sb = (s + router_bias.astype(jnp.float32)).reshape(t, N_GROUPS, EXPERTS_PER_GROUP)
    gscore = jnp.sum(lax.top_k(sb, TOP_K)[0], axis=-1)
    gsel = jnp.argmax(gscore, axis=-1)
    in_grp = sb[jnp.arange(t), gsel]
    _, loc = lax.top_k(in_grp, TOP_K)
    eidx = gsel[:, None] * EXPERTS_PER_GROUP + loc
    w = jnp.take_along_axis(s, eidx, axis=-1)
    w = w / jnp.sum(w, axis=-1, keepdims=True)
    return eidx, w


def moe_experts(x2d, eidx, gate_w, w_gate, w_up, w_down):
    t, d = x2d.shape
    a = t * TOP_K
    flat_e = eidx.reshape(a)
    order = jnp.argsort(flat_e)
    sorted_e = flat_e[order]
    counts = jnp.zeros((N_EXPERTS,), jnp.int32).at[flat_e].add(1)
    padded = (counts + MOE_BLOCK - 1) // MOE_BLOCK * MOE_BLOCK
    pad_end = jnp.cumsum(padded)
    pad_start = pad_end - padded
    start = jnp.cumsum(counts) - counts
    dest = pad_start[sorted_e] + jnp.arange(a, dtype=jnp.int32) - start[sorted_e]
    n_rows = ((a + MOE_BLOCK - 1) // MOE_BLOCK) * MOE_BLOCK + N_EXPERTS * MOE_BLOCK
    n_blk = n_rows // MOE_BLOCK
    tok = jnp.full((n_rows,), t, jnp.int32).at[dest].set((order // TOP_K).astype(jnp.int32))
    x_pad = jnp.concatenate([x2d, jnp.zeros((1, d), x2d.dtype)], axis=0)
    xb = x_pad[tok].reshape(n_blk, MOE_BLOCK, d)
    blk_e = jnp.minimum(jnp.searchsorted(pad_end, jnp.arange(n_blk, dtype=jnp.int32) * MOE_BLOCK,
                                         side='right'), N_EXPERTS - 1)

    def expert_block(args):
        xblk, e = args
        h = jax.nn.silu(xblk @ w_gate[e]) * (xblk @ w_up[e])
        return h @ w_down[e]

    yb = lax.map(expert_block, (xb, blk_e)).reshape(n_rows, d)
    dest_by_assign = jnp.zeros((a,), jnp.int32).at[order].set(dest)
    y_assign = yb[dest_by_assign].reshape(t, TOP_K, d)
    return jnp.sum(y_assign * gate_w[..., None].astype(y_assign.dtype), axis=1)


def encoder_trunk(x, p, a_w_qkv, a_w_o, a_sink, b_w_qkv, b_w_o, b_lambda_q1, b_lambda_k1,
                  b_lambda_q2, b_lambda_k2, b_subln_g, ln1_g, ln1_b, ln2_g, ln2_b,
                  router_w, router_bias, exp_w_gate, exp_w_up, exp_w_down, pl_w_in, pl_w_gate):
    bsz, seq, d = x.shape
    pos = jnp.arange(seq, dtype=jnp.int32)
    for i in range(DEPTH):
        j = i // N_MIXERS
        if i % N_MIXERS == 0:
            h = window_gqa_sink(x, a_w_qkv[j], a_w_o[j], a_sink[j], pos)
        else:
            lambda_init = 0.8 - 0.6 * math.exp(-0.3 * i)
            h = diff_attention(x, b_w_qkv[j], b_w_o[j], b_lambda_q1[j], b_lambda_k1[j],
                               b_lambda_q2[j], b_lambda_k2[j], b_subln_g[j], lambda_init, pos)
        x = layer_norm(ALPHA * x + h, ln1_g[i], ln1_b[i])
        x2d = x.reshape(bsz * seq, d)
        eidx, gw = group_limited_route(x2d, router_w, router_bias)
        m = moe_experts(x2d, eidx, gw, exp_w_gate[i], exp_w_up[i], exp_w_down[i])
        x = layer_norm(ALPHA * x + m.reshape(bsz, seq, d), ln2_g[i], ln2_b[i])
        x = x + jax.nn.sigmoid(x @ pl_w_gate[i]) * (p[i] @ pl_w_in[i])
    return x


def setup_inputs(seed: int = 0) -> dict:
    key = jax.random.key(seed)
    ks = jax.random.split(key, 32)
    f32 = jnp.float32
    n_a = (DEPTH + 1) // 2
    n_b = DEPTH // 2
    D = D_MODEL

    def nrm(k, shape, scale):
        return jax.random.normal(k, shape, f32) * scale

    x_prompt = nrm(ks[0], (BATCH, SEQ, D), 1.0)
    x_sample = nrm(ks[1], (DEC_BATCH, DEC_SEQ, D), 1.0)
    p_prompt = nrm(ks[2], (DEPTH, BATCH, SEQ, PL_DIM), 1.0)
    p_sample = nrm(ks[3], (DEPTH, DEC_BATCH, DEC_SEQ, PL_DIM), 1.0)
    a_qk = nrm(ks[4], (n_a, D, (A_Q_HEADS + A_KV_HEADS) * HEAD_DIM), D ** -0.5)
    a_v = nrm(ks[5], (n_a, D, A_KV_HEADS * HEAD_DIM), D ** -0.5 * BETA)
    a_w_qkv = jnp.concatenate([a_qk, a_v], axis=-1)
    a_w_o = nrm(ks[6], (n_a, A_Q_HEADS * HEAD_DIM, D), (A_Q_HEADS * HEAD_DIM) ** -0.5 * BETA)
    a_sink = nrm(ks[7], (n_a, A_Q_HEADS), 0.5)
    b_qk = nrm(ks[8], (n_b, D, 2 * B_HEADS * 2 * B_QK_DIM), D ** -0.5)
    b_v = nrm(ks[9], (n_b, D, B_HEADS * B_V_DIM), D ** -0.5 * BETA)
    b_w_qkv = jnp.concatenate([b_qk, b_v], axis=-1)
    b_w_o = nrm(ks[10], (n_b, B_HEADS * B_V_DIM, D), (B_HEADS * B_V_DIM) ** -0.5 * BETA)
    b_lambda_q1 = nrm(ks[11], (n_b, B_QK_DIM), 0.1)
    b_lambda_k1 = nrm(ks[12], (n_b, B_QK_DIM), 0.1)
    b_lambda_q2 = nrm(ks[13], (n_b, B_QK_DIM), 0.1)
    b_lambda_k2 = nrm(ks[14], (n_b, B_QK_DIM), 0.1)
    b_subln_g = 1.0 + nrm(ks[15], (n_b, B_V_DIM), 0.01)
    ln1_g = 1.0 + nrm(ks[16], (DEPTH, D), 0.01)
    ln1_b = nrm(ks[17], (DEPTH, D), 0.01)
    ln2_g = 1.0 + nrm(ks[18], (DEPTH, D), 0.01)
    ln2_b = nrm(ks[19], (DEPTH, D), 0.01)
    router_w = nrm(ks[20], (D, N_EXPERTS), D ** -0.5)
    router_bias = nrm(ks[21], (N_EXPERTS,), 0.01)
    exp_w_gate = nrm(ks[22], (DEPTH, N_EXPERTS, D, D_EXPERT), D ** -0.5)
    exp_w_up = nrm(ks[23], (DEPTH, N_EXPERTS, D, D_EXPERT), D ** -0.5)
    exp_w_down = nrm(ks[24], (DEPTH, N_EXPERTS, D_EXPERT, D), D_EXPERT ** -0.5 * BETA)
    pl_w_in = nrm(ks[25], (DEPTH, PL_DIM, D), PL_DIM ** -0.5)
    pl_w_gate = nrm(ks[26], (DEPTH, D, D), D ** -0.5)
    return {"x_prompt": x_prompt, "x_sample": x_sample, "p_prompt": p_prompt, "p_sample": p_sample,
            "a_w_qkv": a_w_qkv, "a_w_o": a_w_o, "a_sink": a_sink,
            "b_w_qkv": b_w_qkv, "b_w_o": b_w_o, "b_lambda_q1": b_lambda_q1,
            "b_lambda_k1": b_lambda_k1, "b_lambda_q2": b_lambda_q2, "b_lambda_k2": b_lambda_k2,
            "b_subln_g": b_subln_g, "ln1_g": ln1_g, "ln1_b": ln1_b, "ln2_g": ln2_g, "ln2_b": ln2_b,
            "router_w": router_w, "router_bias": router_bias, "exp_w_gate": exp_w_gate,
            "exp_w_up": exp_w_up, "exp_w_down": exp_w_down, "pl_w_in": pl_w_in, "pl_w_gate": pl_w_gate}


def reference(x_prompt, x_sample, p_prompt, p_sample, a_w_qkv, a_w_o, a_sink, b_w_qkv, b_w_o,
              b_lambda_q1, b_lambda_k1, b_lambda_q2, b_lambda_k2, b_subln_g, ln1_g, ln1_b,
              ln2_g, ln2_b, router_w, router_bias, exp_w_gate, exp_w_up, exp_w_down,
              pl_w_in, pl_w_gate):
    y_prompt = encoder_trunk(x_prompt, p_prompt, a_w_qkv, a_w_o, a_sink, b_w_qkv, b_w_o,
                             b_lambda_q1, b_lambda_k1, b_lambda_q2, b_lambda_k2, b_subln_g,
                             ln1_g, ln1_b, ln2_g, ln2_b, router_w, router_bias,
                             exp_w_gate, exp_w_up, exp_w_down, pl_w_in, pl_w_gate)
    y_sample = encoder_trunk(x_sample, p_sample, a_w_qkv, a_w_o, a_sink, b_w_qkv, b_w_o,
                             b_lambda_q1, b_lambda_k1, b_lambda_q2, b_lambda_k2, b_subln_g,
                             ln1_g, ln1_b, ln2_g, ln2_b, router_w, router_bias,
                             exp_w_gate, exp_w_up, exp_w_down, pl_w_in, pl_w_gate)
    return (y_prompt, y_sample)
```

```python
import functools
import math

import jax
import jax.numpy as jnp
from jax import lax
from jax.experimental import pallas as pl
from jax.experimental.pallas import tpu as pltpu

F32 = jnp.float32
BF16 = jnp.bfloat16

HEAD_DIM = 128
A_KV_HEADS = 4
WINDOW = 128
ROPE_THETA = 500000.0
ROPE_FRACTION = 4
N_EXPERTS = 16
N_GROUPS = 4
EXPERTS_PER_GROUP = N_EXPERTS // N_GROUPS
TOP_K = 2
N_MIXERS = 2
LN_EPS = 1e-5
NEG_INF = -1e30

LANES = 128
VMEM_LIMIT = 56 * 1024 * 1024

QKV_TM = 1024
QKV_TN = 512
ATT_BLOCK = 128
DIFF_TQ = 256
ROW_TM = 512
MOE_TM = 256
CMB_TM = 256


def _params(*semantics):
    return pltpu.CompilerParams(dimension_semantics=semantics, vmem_limit_bytes=VMEM_LIMIT)


def _rope_tables(seq, head_dim, q_scale):
    rd = head_dim // ROPE_FRACTION
    half = rd // 2
    inv_freq = ROPE_THETA ** (-jnp.arange(half, dtype=F32) / half)
    ang = jnp.arange(seq, dtype=F32)[:, None] * inv_freq[None, :]
    cos, sin = jnp.cos(ang), jnp.sin(ang)
    ones = jnp.ones((seq, head_dim - rd), F32)
    zeros_half = jnp.zeros((seq, half), F32)
    zeros_rest = jnp.zeros((seq, head_dim - rd), F32)
    c = jnp.concatenate([cos, cos, ones], axis=1)
    s_up = jnp.concatenate([-sin, zeros_half, zeros_rest], axis=1)
    s_dn = jnp.concatenate([zeros_half, sin, zeros_rest], axis=1)
    reps = LANES // head_dim
    c, s_up, s_dn = (jnp.tile(t, (1, reps)) for t in (c, s_up, s_dn))
    ident = (jnp.ones_like(c), jnp.zeros_like(c), jnp.zeros_like(c))
    rope_k = (c, s_up, s_dn)
    rope_q = tuple(t * q_scale for t in rope_k)
    return tuple(jnp.stack([rope_q[j], rope_k[j], ident[j]]) for j in range(3))


def _qkv_rope_kernel(x_ref, w_ref, c_ref, su_ref, sd_ref, o_ref, xb_ref, *, half, n_rot_tiles):
    n = pl.program_id(1)

    @pl.when(n == 0)
    def _():
        xb_ref[...] = x_ref[...].astype(BF16)

    acc = jnp.dot(xb_ref[...], w_ref[...], preferred_element_type=F32)
    tn = acc.shape[1]

    @pl.when(n < n_rot_tiles)
    def _():
        reps = tn // LANES
        c = jnp.tile(c_ref[0], (1, reps))
        su = jnp.tile(su_ref[0], (1, reps))
        sd = jnp.tile(sd_ref[0], (1, reps))
        up = pltpu.roll(acc, tn - half, axis=1)
        dn = pltpu.roll(acc, half, axis=1)
        o_ref[...] = (acc * c + up * su + dn * sd).astype(o_ref.dtype)

    @pl.when(n >= n_rot_tiles)
    def _():
        o_ref[...] = acc.astype(o_ref.dtype)


def _qkv_rope(x, w, tables, *, seq, head_dim, n_q_cols, n_k_cols):
    t, d = x.shape
    n_cols = w.shape[1]
    tm = min(QKV_TM, seq)
    tn = QKV_TN
    q_tiles = n_q_cols // tn
    k_tiles = n_k_cols // tn
    half = head_dim // ROPE_FRACTION // 2
    seq_tiles = seq // tm
    assert seq % tm == 0 and n_q_cols % tn == 0 and n_k_cols % tn == 0 and n_cols % tn == 0

    def tab_map(m, n):
        kind = jnp.where(n < q_tiles, 0, jnp.where(n < q_tiles + k_tiles, 1, 2))
        return (kind, m % seq_tiles, 0)

    tab_spec = pl.BlockSpec((1, tm, LANES), tab_map)
    return pl.pallas_call(
        functools.partial(_qkv_rope_kernel, half=half, n_rot_tiles=q_tiles + k_tiles),
        grid=(t // tm, n_cols // tn),
        in_specs=[
            pl.BlockSpec((tm, d), lambda m, n: (m, 0)),
            pl.BlockSpec((d, tn), lambda m, n: (0, n)),
            tab_spec, tab_spec, tab_spec,
        ],
        out_specs=pl.BlockSpec((tm, tn), lambda m, n: (m, n)),
        out_shape=jax.ShapeDtypeStruct((t, n_cols), BF16),
        scratch_shapes=[pltpu.VMEM((tm, d), BF16)],
        compiler_params=_params("parallel", "arbitrary"),
        name="qkv_rope",
    )(x, w, *tables)


def _window_attn_kernel(sink_ref, q_ref, kp_ref, kc_ref, kn_ref, vp_ref, vc_ref, vn_ref, o_ref,
                        *, n_q_heads):
    n = pl.program_id(1)
    nb = pl.num_programs(1)
    blk = ATT_BLOCK
    group = n_q_heads // A_KV_HEADS
    r = lax.broadcasted_iota(jnp.int32, (blk, 3 * blk), 0)
    c = lax.broadcasted_iota(jnp.int32, (blk, 3 * blk), 1)
    rel = c - r
    valid = (rel >= 0) & (rel <= 2 * WINDOW)
    valid = valid & ((c >= blk) | (n > 0)) & ((c < 2 * blk) | (n < nb - 1))
    for h in range(A_KV_HEADS):
        cols = slice(h * HEAD_DIM, (h + 1) * HEAD_DIM)
        kh = jnp.concatenate([kp_ref[:, cols], kc_ref[:, cols], kn_ref[:, cols]], axis=0)
        vh = jnp.concatenate([vp_ref[:, cols], vc_ref[:, cols], vn_ref[:, cols]], axis=0)
        for g in range(group):
            head = h * group + g
            hc = slice(head * HEAD_DIM, (head + 1) * HEAD_DIM)
            s = lax.dot_general(q_ref[:, hc], kh, (((1,), (1,)), ((), ())),
                                preferred_element_type=F32)
            s = jnp.where(valid, s, NEG_INF)
            sink = sink_ref[head]
            m = jnp.maximum(jnp.max(s, axis=-1, keepdims=True), sink)
            e = jnp.exp(s - m)
            den = jnp.sum(e, axis=-1, keepdims=True) + jnp.exp(sink - m)
            o = jnp.dot(e.astype(BF16), vh, preferred_element_type=F32)
            o_ref[:, hc] = (o / den).astype(o_ref.dtype)


def _window_attention(qkv, sink, *, bsz, seq, n_q_heads):
    t = qkv.shape[0]
    blk = ATT_BLOCK
    nb = seq // blk
    dq = n_q_heads * HEAD_DIM
    dk = A_KV_HEADS * HEAD_DIM
    k_col = dq // dk
    v_col = k_col + 1

    def kv_spec(col, shift):
        def index(b, n):
            return (b * nb + jnp.clip(n + shift, 0, nb - 1), col)
        return pl.BlockSpec((blk, dk), index)

    return pl.pallas_call(
        functools.partial(_window_attn_kernel, n_q_heads=n_q_heads),
        grid=(bsz, nb),
        in_specs=[
            pl.BlockSpec(memory_space=pltpu.SMEM),
            pl.BlockSpec((blk, dq), lambda b, n: (b * nb + n, 0)),
            kv_spec(k_col, -1), kv_spec(k_col, 0), kv_spec(k_col, 1),
            kv_spec(v_col, -1), kv_spec(v_col, 0), kv_spec(v_col, 1),
        ],
        out_specs=pl.BlockSpec((blk, dq), lambda b, n: (b * nb + n, 0)),
        out_shape=jax.ShapeDtypeStruct((t, dq), BF16),
        compiler_params=_params("parallel", "arbitrary"),
        name="window_attention",
    )(sink, qkv, qkv, qkv, qkv, qkv, qkv, qkv)


def _diff_attn_kernel(q_ref, k_ref, v_ref, lam_ref, g_ref, o_ref, vext_ref, *, lambda_init):
    qi = pl.program_id(2)
    tq = q_ref.shape[0]
    dv = v_ref.shape[1]

    @pl.when(qi == 0)
    def _():
        vext_ref[:, :dv] = v_ref[...]
        vext_ref[:, dv:] = jnp.ones((v_ref.shape[0], vext_ref.shape[1] - dv), BF16)

    q = q_ref[...]
    lane = lax.broadcasted_iota(jnp.int32, q.shape, 1)
    zero = jnp.zeros_like(q)
    qq = jnp.concatenate([jnp.where(lane < dv // 2, q, zero), jnp.where(lane >= dv // 2, q, zero)],
                         axis=0)
    s = lax.dot_general(qq, k_ref[...], (((1,), (1,)), ((), ())), preferred_element_type=F32)
    m = jnp.max(s, axis=-1, keepdims=True)
    e = jnp.exp(s - m).astype(BF16)
    ov = jnp.dot(e, vext_ref[...], preferred_element_type=F32)
    o = ov[:, :dv] / ov[:, dv:dv + 1]
    lp = lam_ref[...]
    lam = (jnp.exp(jnp.sum(lp[0:1] * lp[1:2], axis=-1, keepdims=True))
           - jnp.exp(jnp.sum(lp[2:3] * lp[3:4], axis=-1, keepdims=True)) + lambda_init)
    a = o[:tq] - lam * o[tq:]
    a = a * lax.rsqrt(jnp.mean(a * a, axis=-1, keepdims=True) + LN_EPS) * g_ref[...]
    o_ref[...] = (a * (1.0 - lambda_init)).astype(o_ref.dtype)


def _diff_attention(qkv, lam_params, subln_g, *, bsz, seq, n_heads, lambda_init):
    t = qkv.shape[0]
    tq = min(DIFF_TQ, seq)
    nq = seq // tq
    dv = HEAD_DIM
    return pl.pallas_call(
        functools.partial(_diff_attn_kernel, lambda_init=lambda_init),
        grid=(bsz, n_heads, nq),
        in_specs=[
            pl.BlockSpec((tq, dv), lambda b, h, i: (b * nq + i, h)),
            pl.BlockSpec((seq, dv), lambda b, h, i: (b, n_heads + h)),
            pl.BlockSpec((seq, dv), lambda b, h, i: (b, 2 * n_heads + h)),
            pl.BlockSpec(lam_params.shape, lambda b, h, i: (0, 0)),
            pl.BlockSpec(subln_g.shape, lambda b, h, i: (0, 0)),
        ],
        out_specs=pl.BlockSpec((tq, dv), lambda b, h, i: (b * nq + i, h)),
        out_shape=jax.ShapeDtypeStruct((t, n_heads * dv), BF16),
        scratch_shapes=[pltpu.VMEM((seq, 2 * dv), BF16)],
        compiler_params=_params("parallel", "parallel", "arbitrary"),
        name="diff_attention",
    )(qkv, qkv, qkv, lam_params, subln_g)


def _layer_norm(y, g, b):
    yc = y - jnp.mean(y, axis=-1, keepdims=True)
    var = jnp.mean(yc * yc, axis=-1, keepdims=True)
    return yc * lax.rsqrt(var + LN_EPS) * g + b


def _route(scores, biased):
    sb = [biased[e:e + 1] for e in range(N_EXPERTS)]
    sc = [scores[e:e + 1] for e in range(N_EXPERTS)]
    epg = EXPERTS_PER_GROUP
    gscore = []
    for g in range(N_GROUPS):
        v = sb[g * epg:(g + 1) * epg]
        best = None
        for i in range(epg):
            for j in range(i + 1, epg):
                pair = v[i] + v[j]
                best = pair if best is None else jnp.maximum(best, pair)
        gscore.append(best)
    gbest = gscore[0]
    gsel = jnp.zeros_like(gbest, dtype=jnp.int32)
    for g in range(1, N_GROUPS):
        upd = gscore[g] > gbest
        gbest = jnp.where(upd, gscore[g], gbest)
        gsel = jnp.where(upd, g, gsel)

    def pick_group(rows, j):
        out = rows[j]
        for g in range(1, N_GROUPS):
            out = jnp.where(gsel == g, rows[g * epg + j], out)
        return out

    v = [pick_group(sb, j) for j in range(epg)]
    u = [pick_group(sc, j) for j in range(epg)]
    b1, i1 = v[0], jnp.zeros_like(gsel)
    for j in range(1, epg):
        upd = v[j] > b1
        b1 = jnp.where(upd, v[j], b1)
        i1 = jnp.where(upd, j, i1)
    b2 = jnp.full_like(b1, -jnp.inf)
    i2 = jnp.zeros_like(gsel)
    for j in range(epg):
        upd = (i1 != j) & (v[j] > b2)
        b2 = jnp.where(upd, v[j], b2)
        i2 = jnp.where(upd, j, i2)

    def pick_local(idx):
        out = u[0]
        for j in range(1, epg):
            out = jnp.where(idx == j, u[j], out)
        return out

    w1, w2 = pick_local(i1), pick_local(i2)
    wsum = w1 + w2
    return (gsel * epg + i1, gsel * epg + i2), (w1 / wsum, w2 / wsum)


def _oproj_ln_route_kernel(o_ref, wo_ref, x_ref, g_ref, b_ref, rw_ref, rb_ref,
                           x1_ref, eidx_ref, gw_ref, *, alpha):
    h = jnp.dot(o_ref[...], wo_ref[...], preferred_element_type=F32)
    xn = _layer_norm(alpha * x_ref[...] + h, g_ref[...], b_ref[...])
    x1_ref[...] = xn
    logits = lax.dot_general(rw_ref[...], xn, (((1,), (1,)), ((), ())),
                             precision=lax.Precision.HIGHEST, preferred_element_type=F32)
    scores = jax.nn.sigmoid(logits)
    (e1, e2), (w1, w2) = _route(scores, scores + rb_ref[...])
    eidx_ref[0:1, :] = e1
    eidx_ref[1:2, :] = e2
    gw_ref[0:1, :] = w1
    gw_ref[1:2, :] = w2


def _oproj_ln_route(o, wo, x, ln_g, ln_b, router_wt, router_b, *, alpha):
    t, d = x.shape
    tm = min(ROW_TM, t)
    assert t % tm == 0
    row = pl.BlockSpec((tm, d), lambda m: (m, 0))
    full = lambda a: pl.BlockSpec(a.shape, lambda m: (0,) * a.ndim)
    sel = pl.BlockSpec((TOP_K, tm), lambda m: (0, m))
    return pl.pallas_call(
        functools.partial(_oproj_ln_route_kernel, alpha=alpha),
        grid=(t // tm,),
        in_specs=[row, full(wo), row, full(ln_g), full(ln_b), full(router_wt), full(router_b)],
        out_specs=[row, sel, sel],
        out_shape=[jax.ShapeDtypeStruct((t, d), F32),
                   jax.ShapeDtypeStruct((TOP_K, t), jnp.int32),
                   jax.ShapeDtypeStruct((TOP_K, t), F32)],
        compiler_params=_params("parallel"),
        name="oproj_ln_route",
    )(o, wo, x, ln_g, ln_b, router_wt, router_b)


def _dispatch_plan(eidx, tm):
    t = eidx.shape[1]
    a = TOP_K * t
    flat_e = eidx.reshape(a)
    onehot = (flat_e[:, None] == jnp.arange(N_EXPERTS, dtype=jnp.int32)[None, :]).astype(jnp.int32)
    csum = jnp.cumsum(onehot, axis=0)
    counts = csum[-1]
    rank = jnp.sum(onehot * csum, axis=1) - 1
    padded = (counts + tm - 1) // tm * tm
    pad_end = jnp.cumsum(padded)
    pad_start = pad_end - padded
    start = jnp.cumsum(counts) - counts
    dest = jnp.sum(onehot * pad_start[None, :], axis=1) + rank
    n_rows = (a + tm - 1) // tm * tm + N_EXPERTS * tm
    n_blk = n_rows // tm
    blk_e = jnp.minimum(jnp.searchsorted(pad_end, jnp.arange(n_blk, dtype=jnp.int32) * tm,
                                         side='right'), N_EXPERTS - 1).astype(jnp.int32)
    n_used = (pad_end[-1] // tm).astype(jnp.int32).reshape(1)
    order = jnp.argsort(flat_e, stable=True).astype(jnp.int32)
    row = jnp.arange(n_rows, dtype=jnp.int32)
    row_e = jnp.repeat(blk_e, tm)
    j = row - pad_start[row_e]
    valid = (j >= 0) & (j < counts[row_e])
    src = order[jnp.clip(start[row_e] + j, 0, a - 1)] % t
    tok = jnp.where(valid, src, 0).astype(jnp.int32)
    return tok, dest.astype(jnp.int32), blk_e, n_used


def _moe_kernel(blk_e_ref, nused_ref, tok_ref, x_hbm, wg_ref, wu_ref, wd_ref, y_ref, xbuf, sem):
    del blk_e_ref
    i = pl.program_id(0)
    tm = y_ref.shape[0]
    n_used = nused_ref[0]
    slot = i % 2

    def row_copy(blk, r, sl):
        tok = tok_ref[blk * tm + r]
        return pltpu.make_async_copy(x_hbm.at[pl.ds(tok, 1)], xbuf.at[sl, pl.ds(r, 1)],
                                     sem.at[sl])

    def start_rows(blk, sl):
        def body(r, carry):
            row_copy(blk, r, sl).start()
            return carry
        lax.fori_loop(0, tm, body, 0)

    def wait_rows(blk, sl):
        def body(r, carry):
            row_copy(blk, r, sl).wait()
            return carry
        lax.fori_loop(0, tm, body, 0)

    @pl.when(i == 0)
    def _():
        start_rows(0, 0)

    @pl.when(i + 1 < n_used)
    def _():
        start_rows(i + 1, 1 - slot)

    @pl.when(i < n_used)
    def _():
        wait_rows(i, slot)
        x = xbuf[slot].astype(BF16)
        gate = jnp.dot(x, wg_ref[...], preferred_element_type=F32)
        up = jnp.dot(x, wu_ref[...], preferred_element_type=F32)
        h = (gate * jax.nn.sigmoid(gate) * up).astype(BF16)
        y_ref[...] = jnp.dot(h, wd_ref[...], preferred_element_type=F32)

    @pl.when(i >= n_used)
    def _():
        y_ref[...] = jnp.zeros_like(y_ref)


def _moe_experts(x, tok, blk_e, n_used, w_gate, w_up, w_down):
    d = x.shape[1]
    f = w_gate.shape[2]
    n_rows = tok.shape[0]
    tm = MOE_TM
    grid_spec = pltpu.PrefetchScalarGridSpec(
        num_scalar_prefetch=3,
        grid=(n_rows // tm,),
        in_specs=[
            pl.BlockSpec(memory_space=pl.ANY),
            pl.BlockSpec((None, d, f), lambda i, be, nu, tk: (be[i], 0, 0)),
            pl.BlockSpec((None, d, f), lambda i, be, nu, tk: (be[i], 0, 0)),
            pl.BlockSpec((None, f, d), lambda i, be, nu, tk: (be[i], 0, 0)),
        ],
        out_specs=pl.BlockSpec((tm, d), lambda i, be, nu, tk: (i, 0)),
        scratch_shapes=[pltpu.VMEM((2, tm, d), F32), pltpu.SemaphoreType.DMA((2,))],
    )
    return pl.pallas_call(
        _moe_kernel,
        grid_spec=grid_spec,
        out_shape=jax.ShapeDtypeStruct((n_rows, d), F32),
        compiler_params=_params("arbitrary"),
        name="moe_experts",
    )(blk_e, n_used, tok, x, w_gate, w_up, w_down)


def _combine_kernel(dest_ref, y_hbm, gw_ref, x1_ref, g_ref, b_ref, p_ref, wpg_ref, wpi_ref,
                    out_ref, ybuf, sem, *, alpha, n_tokens):
    i = pl.program_id(0)
    n = pl.num_programs(0)
    tm = out_ref.shape[0]
    slot = i % 2

    def row_copy(blk, k, r, sl):
        row = dest_ref[k * n_tokens + blk * tm + r]
        return pltpu.make_async_copy(y_hbm.at[pl.ds(row, 1)], ybuf.at[sl, k, pl.ds(r, 1)],
                                     sem.at[sl])

    def start_rows(blk, sl):
        def body(r, carry):
            for k in range(TOP_K):
                row_copy(blk, k, r, sl).start()
            return carry
        lax.fori_loop(0, tm, body, 0)

    def wait_rows(blk, sl):
        def body(r, carry):
            for k in range(TOP_K):
                row_copy(blk, k, r, sl).wait()
            return carry
        lax.fori_loop(0, tm, body, 0)

    @pl.when(i == 0)
    def _():
        start_rows(0, 0)

    @pl.when(i + 1 < n)
    def _():
        start_rows(i + 1, 1 - slot)

    wait_rows(i, slot)
    gw = gw_ref[...]
    moe = ybuf[slot, 0] * gw[:, 0:1] + ybuf[slot, 1] * gw[:, 1:2]
    x2 = _layer_norm(alpha * x1_ref[...] + moe, g_ref[...], b_ref[...])
    gate = jax.nn.sigmoid(jnp.dot(x2.astype(BF16), wpg_ref[...], preferred_element_type=F32))
    emb = jnp.dot(p_ref[...].astype(BF16), wpi_ref[...], preferred_element_type=F32)
    out_ref[...] = x2 + gate * emb


def _combine_ln_gate(y, dest, gw_t, x1, ln_g, ln_b, p, w_pl_gate, w_pl_in, *, alpha):
    t, d = x1.shape
    tm = min(CMB_TM, t)
    assert t % tm == 0
    row = lambda width: pl.BlockSpec((tm, width), lambda m, ds: (m, 0))
    full = lambda a: pl.BlockSpec(a.shape, lambda m, ds: (0,) * a.ndim)
    grid_spec = pltpu.PrefetchScalarGridSpec(
        num_scalar_prefetch=1,
        grid=(t // tm,),
        in_specs=[
            pl.BlockSpec(memory_space=pl.ANY),
            row(TOP_K), row(d), full(ln_g), full(ln_b), row(p.shape[1]),
            full(w_pl_gate), full(w_pl_in),
        ],
        out_specs=row(d),
        scratch_shapes=[pltpu.VMEM((2, TOP_K, tm, d), F32), pltpu.SemaphoreType.DMA((2,))],
    )
    return pl.pallas_call(
        functools.partial(_combine_kernel, alpha=alpha, n_tokens=t),
        grid_spec=grid_spec,
        out_shape=jax.ShapeDtypeStruct((t, d), F32),
        compiler_params=_params("arbitrary"),
        name="combine_ln_gate",
    )(dest, y, gw_t, x1, ln_g, ln_b, p, w_pl_gate, w_pl_in)


def kernel(x_prompt, x_sample, p_prompt, p_sample, a_w_qkv, a_w_o, a_sink, b_w_qkv, b_w_o, b_lambda_q1, b_lambda_k1, b_lambda_q2, b_lambda_k2, b_subln_g, ln1_g, ln1_b, ln2_g, ln2_b, router_w, router_bias, exp_w_gate, exp_w_up, exp_w_down, pl_w_in, pl_w_gate):
    depth = ln1_g.shape[0]
    d = x_prompt.shape[-1]
    seq = x_prompt.shape[1]
    assert x_sample.shape[1] == seq
    n_prompt = x_prompt.shape[0]
    bsz = n_prompt + x_sample.shape[0]
    t = bsz * seq
    n_heads = d // HEAD_DIM
    alpha = (2.0 * depth) ** 0.25

    x = jnp.concatenate([x_prompt, x_sample], axis=0).reshape(t, d)
    p = jnp.concatenate([p_prompt, p_sample], axis=1).reshape(depth, t, p_prompt.shape[-1])

    a_tables = _rope_tables(seq, HEAD_DIM, HEAD_DIM ** -0.5)
    b_tables = _rope_tables(seq, HEAD_DIM // 2, (HEAD_DIM // 2) ** -0.5)
    router_wt = router_w.T.astype(F32)
    router_b = router_bias.astype(F32).reshape(N_EXPERTS, 1)
    row2d = lambda v: v.astype(F32).reshape(1, -1)

    for i in range(depth):
        j = i // N_MIXERS
        if i % N_MIXERS == 0:
            qkv = _qkv_rope(x, a_w_qkv[j].astype(BF16), a_tables, seq=seq, head_dim=HEAD_DIM,
                            n_q_cols=n_heads * HEAD_DIM, n_k_cols=A_KV_HEADS * HEAD_DIM)
            o = _window_attention(qkv, a_sink[j].astype(F32), bsz=bsz, seq=seq, n_q_heads=n_heads)
            w_o = a_w_o[j]
        else:
            lambda_init = 0.8 - 0.6 * math.exp(-0.3 * i)
            qkv = _qkv_rope(x, b_w_qkv[j].astype(BF16), b_tables, seq=seq, head_dim=HEAD_DIM // 2,
                            n_q_cols=n_heads * HEAD_DIM, n_k_cols=n_heads * HEAD_DIM)
            lam_params = jnp.stack([b_lambda_q1[j], b_lambda_k1[j], b_lambda_q2[j],
                                    b_lambda_k2[j]]).astype(F32)
            o = _diff_attention(qkv, lam_params, row2d(b_subln_g[j]), bsz=bsz, seq=seq,
                                n_heads=n_heads, lambda_init=lambda_init)
            w_o = b_w_o[j]
        x1, eidx, gw = _oproj_ln_route(o, w_o.astype(BF16), x, row2d(ln1_g[i]), row2d(ln1_b[i]),
                                       router_wt, router_b, alpha=alpha)
        tok, dest, blk_e, n_used = _dispatch_plan(eidx, MOE_TM)
        y = _moe_experts(x1, tok, blk_e, n_used, exp_w_gate[i].astype(BF16),
                         exp_w_up[i].astype(BF16), exp_w_down[i].astype(BF16))
        x = _combine_ln_gate(y, dest, gw.T, x1, row2d(ln2_g[i]), row2d(ln2_b[i]), p[i],
                             pl_w_gate[i].astype(BF16), pl_w_in[i].astype(BF16), alpha=alpha)

    x = x.reshape(bsz, seq, d)
    return (x[:n_prompt], x[n_prompt:])
```

```python
import functools
import math

import jax
import jax.numpy as jnp
from jax import lax
from jax.experimental import pallas as pl
from jax.experimental.pallas import tpu as pltpu

F32 = jnp.float32
BF16 = jnp.bfloat16

HEAD_DIM = 128
A_KV_HEADS = 4
WINDOW = 128
ROPE_THETA = 500000.0
ROPE_FRACTION = 4
N_EXPERTS = 16
N_GROUPS = 4
EXPERTS_PER_GROUP = N_EXPERTS // N_GROUPS
TOP_K = 2
N_MIXERS = 2
LN_EPS = 1e-5
NEG_INF = -1e30

LANES = 128
VMEM_LIMIT = 56 * 1024 * 1024

QKV_TM = 1024
QKV_TN = 512
QKV_SUB = 256
ATT_BLOCK = 128
DIFF_TQ = 512
DIFF_COLS = 256
DIFF_KEY_CHUNK = 1024
ROW_TM = 512
MOE_TM = 256
CMB_TM = 256


def _params(*semantics):
    return pltpu.CompilerParams(dimension_semantics=semantics, vmem_limit_bytes=VMEM_LIMIT)


def _rope_tables(seq, head_dim, q_scale):
    rd = head_dim // ROPE_FRACTION
    half = rd // 2
    inv_freq = ROPE_THETA ** (-jnp.arange(half, dtype=F32) / half)
    ang = jnp.arange(seq, dtype=F32)[:, None] * inv_freq[None, :]
    cos, sin = jnp.cos(ang), jnp.sin(ang)
    ones = jnp.ones((seq, head_dim - rd), F32)
    zeros_half = jnp.zeros((seq, half), F32)
    zeros_rest = jnp.zeros((seq, head_dim - rd), F32)
    c = jnp.concatenate([cos, cos, ones], axis=1)
    s_up = jnp.concatenate([-sin, zeros_half, zeros_rest], axis=1)
    s_dn = jnp.concatenate([zeros_half, sin, zeros_rest], axis=1)
    reps = LANES // head_dim
    c, s_up, s_dn = (jnp.tile(t, (1, reps)) for t in (c, s_up, s_dn))
    ident = (jnp.ones_like(c), jnp.zeros_like(c), jnp.zeros_like(c))
    rope_k = (c, s_up, s_dn)
    rope_q = tuple(t * q_scale for t in rope_k)
    return tuple(jnp.stack([rope_q[j], rope_k[j], ident[j]]) for j in range(3))


def _qkv_rope_kernel(x_ref, w_ref, c_ref, su_ref, sd_ref, o_ref, xb_ref, *, half, n_rot_tiles):
    n = pl.program_id(1)
    tm, tn = o_ref.shape
    sub = min(QKV_SUB, tm)

    @pl.when(n == 0)
    def _():
        xb_ref[...] = x_ref[...].astype(BF16)

    def tile_loop(epilogue):
        def mm(r):
            return jnp.dot(xb_ref[r * sub:(r + 1) * sub, :], w_ref[...], preferred_element_type=F32)
        nxt = mm(0)
        for r in range(tm // sub):
            acc = nxt
            if r + 1 < tm // sub:
                nxt = mm(r + 1)
            o_ref[r * sub:(r + 1) * sub, :] = epilogue(acc, r).astype(o_ref.dtype)

    def rope(acc, r):
        rows = slice(r * sub, (r + 1) * sub)
        reps = tn // LANES
        c = jnp.tile(c_ref[0, rows, :], (1, reps))
        su = jnp.tile(su_ref[0, rows, :], (1, reps))
        sd = jnp.tile(sd_ref[0, rows, :], (1, reps))
        up = pltpu.roll(acc, tn - half, axis=1)
        dn = pltpu.roll(acc, half, axis=1)
        return acc * c + up * su + dn * sd

    @pl.when(n < n_rot_tiles)
    def _():
        tile_loop(rope)

    @pl.when(n >= n_rot_tiles)
    def _():
        tile_loop(lambda acc, r: acc)


def _qkv_rope(x, w, tables, *, seq, head_dim, n_q_cols, n_k_cols):
    t, d = x.shape
    n_cols = w.shape[1]
    tm = min(QKV_TM, seq)
    tn = QKV_TN
    q_tiles = n_q_cols // tn
    k_tiles = n_k_cols // tn
    half = head_dim // ROPE_FRACTION // 2
    seq_tiles = seq // tm
    assert seq % tm == 0 and n_q_cols % tn == 0 and n_k_cols % tn == 0 and n_cols % tn == 0

    def tab_map(m, n):
        kind = jnp.where(n < q_tiles, 0, jnp.where(n < q_tiles + k_tiles, 1, 2))
        return (kind, m % seq_tiles, 0)

    tab_spec = pl.BlockSpec((1, tm, LANES), tab_map)
    return pl.pallas_call(
        functools.partial(_qkv_rope_kernel, half=half, n_rot_tiles=q_tiles + k_tiles),
        grid=(t // tm, n_cols // tn),
        in_specs=[
            pl.BlockSpec((tm, d), lambda m, n: (m, 0)),
            pl.BlockSpec((d, tn), lambda m, n: (0, n)),
            tab_spec, tab_spec, tab_spec,
        ],
        out_specs=pl.BlockSpec((tm, tn), lambda m, n: (m, n)),
        out_shape=jax.ShapeDtypeStruct((t, n_cols), BF16),
        scratch_shapes=[pltpu.VMEM((tm, d), BF16)],
        compiler_params=_params("parallel", "arbitrary"),
        name="qkv_rope",
    )(x, w, *tables)


def _window_attn_kernel(sink_ref, q_ref, kp_ref, kc_ref, kn_ref, vp_ref, vc_ref, vn_ref, o_ref,
                        *, n_q_heads):
    n = pl.program_id(1)
    nb = pl.num_programs(1)
    blk = ATT_BLOCK
    group = n_q_heads // A_KV_HEADS
    c = lax.broadcasted_iota(jnp.int32, (3 * blk, group * blk), 0)
    r = lax.broadcasted_iota(jnp.int32, (3 * blk, group * blk), 1) % blk
    rel = c - r
    valid = (rel >= 0) & (rel <= 2 * WINDOW)
    valid = valid & ((c >= blk) | (n > 0)) & ((c < 2 * blk) | (n < nb - 1))
    log2e = math.log2(math.e)

    def scores(h):
        cols = slice(h * HEAD_DIM, (h + 1) * HEAD_DIM)
        kh = jnp.concatenate([kp_ref[:, cols], kc_ref[:, cols], kn_ref[:, cols]], axis=0)
        qg = jnp.concatenate([q_ref[:, (h * group + g) * HEAD_DIM:(h * group + g + 1) * HEAD_DIM]
                              for g in range(group)], axis=0)
        st = lax.dot_general(kh, qg, (((1,), (1,)), ((), ())), preferred_element_type=F32)
        return jnp.where(valid, st, NEG_INF)

    nxt = scores(0)
    for h in range(A_KV_HEADS):
        st = nxt
        if h + 1 < A_KV_HEADS:
            nxt = scores(h + 1)
        rows = slice(h * HEAD_DIM, (h + 1) * HEAD_DIM)
        vth = jnp.concatenate([vp_ref[rows, :], vc_ref[rows, :], vn_ref[rows, :]], axis=1)
        sink = jnp.concatenate([jnp.full((1, blk), sink_ref[h * group + g] * log2e, F32)
                                for g in range(group)], axis=1)
        m = jnp.maximum(jnp.max(st, axis=0, keepdims=True), sink)
        e = jnp.exp2(st - m)
        den = jnp.sum(e, axis=0, keepdims=True) + jnp.exp2(sink - m)
        ot = jnp.dot(vth, e.astype(BF16), preferred_element_type=F32) * (1.0 / den)
        for g in range(group):
            head = h * group + g
            o_ref[:, head * HEAD_DIM:(head + 1) * HEAD_DIM] = (
                ot[:, g * blk:(g + 1) * blk].T.astype(o_ref.dtype))


def _window_attention(qk, vt, sink, *, bsz, seq, n_q_heads):
    t = qk.shape[0]
    blk = ATT_BLOCK
    nb = seq // blk
    dq = n_q_heads * HEAD_DIM
    dk = A_KV_HEADS * HEAD_DIM
    k_col = dq // dk

    def shifted(n, shift):
        return jnp.clip(n + shift, 0, nb - 1)

    k_spec = lambda s: pl.BlockSpec((blk, dk), lambda b, n: (b * nb + shifted(n, s), k_col))
    v_spec = lambda s: pl.BlockSpec((dk, blk), lambda b, n: (0, b * nb + shifted(n, s)))
    return pl.pallas_call(
        functools.partial(_window_attn_kernel, n_q_heads=n_q_heads),
        grid=(bsz, nb),
        in_specs=[
            pl.BlockSpec(memory_space=pltpu.SMEM),
            pl.BlockSpec((blk, dq), lambda b, n: (b * nb + n, 0)),
            k_spec(-1), k_spec(0), k_spec(1), v_spec(-1), v_spec(0), v_spec(1),
        ],
        out_specs=pl.BlockSpec((blk, dq), lambda b, n: (b * nb + n, 0)),
        out_shape=jax.ShapeDtypeStruct((t, dq), BF16),
        compiler_params=_params("parallel", "arbitrary"),
        name="window_attention",
    )(sink, qk, qk, qk, qk, vt, vt, vt)


def _vt_proj_kernel(x_ref, wt_ref, o_ref, xb_ref):
    @pl.when(pl.program_id(1) == 0)
    def _():
        xb_ref[...] = x_ref[...].astype(BF16)

    o_ref[...] = lax.dot_general(wt_ref[...], xb_ref[...], (((1,), (1,)), ((), ())),
                                 preferred_element_type=F32).astype(o_ref.dtype)


def _vt_proj(x, w_t, *, seq):
    t, d = x.shape
    n_out = w_t.shape[0]
    tm = min(QKV_TM, seq)
    tn = min(QKV_TN, n_out)
    assert t % tm == 0 and n_out % tn == 0
    return pl.pallas_call(
        _vt_proj_kernel,
        grid=(t // tm, n_out // tn),
        in_specs=[pl.BlockSpec((tm, d), lambda m, n: (m, 0)),
                  pl.BlockSpec((tn, d), lambda m, n: (n, 0))],
        out_specs=pl.BlockSpec((tn, tm), lambda m, n: (n, m)),
        out_shape=jax.ShapeDtypeStruct((n_out, t), BF16),
        scratch_shapes=[pltpu.VMEM((tm, d), BF16)],
        compiler_params=_params("parallel", "arbitrary"),
        name="vt_proj",
    )(x, w_t)


def _diff_attn_kernel(q_ref, k_ref, vt_ref, lam_ref, g_ref, o_ref, *, lambda_init):
    tq, dv = q_ref.shape
    seq = k_ref.shape[0]
    q = q_ref[...]
    lane = lax.broadcasted_iota(jnp.int32, q.shape, 1)
    zero = jnp.zeros_like(q)
    qq = jnp.concatenate([jnp.where(lane < dv // 2, q, zero), jnp.where(lane >= dv // 2, q, zero)],
                         axis=0)
    cols = min(DIFF_COLS, tq)
    chunk = min(DIFF_KEY_CHUNK, seq)
    n_tiles = 2 * tq // cols
    n_chunks = seq // chunk

    def scores(c, j):
        return lax.dot_general(k_ref[j * chunk:(j + 1) * chunk, :], qq[c * cols:(c + 1) * cols],
                               (((1,), (1,)), ((), ())), preferred_element_type=F32)

    def tile_scores(c):
        st = [scores(c, j) for j in range(n_chunks)]
        m = functools.reduce(jnp.maximum, [jnp.max(x, axis=0, keepdims=True) for x in st])
        return st, m

    outs = []
    nxt = tile_scores(0)
    for c in range(n_tiles):
        st, m = nxt
        nxt_st, nxt_m = [], []
        l, acc = None, None
        for j in range(n_chunks):
            if c + 1 < n_tiles:
                nxt_st.append(scores(c + 1, j))
                nxt_m.append(jnp.max(nxt_st[-1], axis=0, keepdims=True))
            e = jnp.exp2(st[j] - m)
            ls = jnp.sum(e, axis=0, keepdims=True)
            pv = jnp.dot(vt_ref[:, j * chunk:(j + 1) * chunk], e.astype(BF16),
                         preferred_element_type=F32)
            l = ls if l is None else l + ls
            acc = pv if acc is None else acc + pv
        if c + 1 < n_tiles:
            nxt = (nxt_st, functools.reduce(jnp.maximum, nxt_m))
        outs.append(acc * (1.0 / l))
    ot = jnp.concatenate(outs, axis=1)
    lp = lam_ref[...]
    lam = (jnp.exp(jnp.sum(lp[0:1] * lp[1:2], axis=-1, keepdims=True))
           - jnp.exp(jnp.sum(lp[2:3] * lp[3:4], axis=-1, keepdims=True)) + lambda_init)
    a = (ot[:, :tq] - lam * ot[:, tq:]).T
    a = a * lax.rsqrt(jnp.mean(a * a, axis=-1, keepdims=True) + LN_EPS) * g_ref[...]
    o_ref[...] = (a * (1.0 - lambda_init)).astype(o_ref.dtype)


def _diff_attention(qk, vt, lam_params, subln_g, *, bsz, seq, n_heads, lambda_init):
    t = qk.shape[0]
    tq = min(DIFF_TQ, seq)
    nq = seq // tq
    dv = HEAD_DIM
    return pl.pallas_call(
        functools.partial(_diff_attn_kernel, lambda_init=lambda_init),
        grid=(bsz, n_heads, nq),
        in_specs=[
            pl.BlockSpec((tq, dv), lambda b, h, i: (b * nq + i, h)),
            pl.BlockSpec((seq, dv), lambda b, h, i: (b, n_heads + h)),
            pl.BlockSpec((dv, seq), lambda b, h, i: (h, b)),
            pl.BlockSpec(lam_params.shape, lambda b, h, i: (0, 0)),
            pl.BlockSpec(subln_g.shape, lambda b, h, i: (0, 0)),
        ],
        out_specs=pl.BlockSpec((tq, dv), lambda b, h, i: (b * nq + i, h)),
        out_shape=jax.ShapeDtypeStruct((t, n_heads * dv), BF16),
        compiler_params=_params("parallel", "parallel", "arbitrary"),
        name="diff_attention",
    )(qk, qk, vt, lam_params, subln_g)


def _layer_norm(y, g, b):
    yc = y - jnp.mean(y, axis=-1, keepdims=True)
    var = jnp.mean(yc * yc, axis=-1, keepdims=True)
    return yc * lax.rsqrt(var + LN_EPS) * g + b


def _route(scores, biased):
    sb = [biased[e:e + 1] for e in range(N_EXPERTS)]
    sc = [scores[e:e + 1] for e in range(N_EXPERTS)]
    epg = EXPERTS_PER_GROUP
    gscore = []
    for g in range(N_GROUPS):
        v = sb[g * epg:(g + 1) * epg]
        best = None
        for i in range(epg):
            for j in range(i + 1, epg):
                pair = v[i] + v[j]
                best = pair if best is None else jnp.maximum(best, pair)
        gscore.append(best)
    gbest = gscore[0]
    gsel = jnp.zeros_like(gbest, dtype=jnp.int32)
    for g in range(1, N_GROUPS):
        upd = gscore[g] > gbest
        gbest = jnp.where(upd, gscore[g], gbest)
        gsel = jnp.where(upd, g, gsel)

    def pick_group(rows, j):
        out = rows[j]
        for g in range(1, N_GROUPS):
            out = jnp.where(gsel == g, rows[g * epg + j], out)
        return out

    v = [pick_group(sb, j) for j in range(epg)]
    u = [pick_group(sc, j) for j in range(epg)]
    b1, i1 = v[0], jnp.zeros_like(gsel)
    for j in range(1, epg):
        upd = v[j] > b1
        b1 = jnp.where(upd, v[j], b1)
        i1 = jnp.where(upd, j, i1)
    b2 = jnp.full_like(b1, -jnp.inf)
    i2 = jnp.zeros_like(gsel)
    for j in range(epg):
        upd = (i1 != j) & (v[j] > b2)
        b2 = jnp.where(upd, v[j], b2)
        i2 = jnp.where(upd, j, i2)

    def pick_local(idx):
        out = u[0]
        for j in range(1, epg):
            out = jnp.where(idx == j, u[j], out)
        return out

    w1, w2 = pick_local(i1), pick_local(i2)
    wsum = w1 + w2
    return (gsel * epg + i1, gsel * epg + i2), (w1 / wsum, w2 / wsum)


def _oproj_ln_route_kernel(o_ref, wo_ref, x_ref, g_ref, b_ref, rw_ref, rb_ref,
                           x1_ref, eidx_ref, gw_ref, *, alpha):
    h = jnp.dot(o_ref[...], wo_ref[...], preferred_element_type=F32)
    xn = _layer_norm(alpha * x_ref[...] + h, g_ref[...], b_ref[...])
    x1_ref[...] = xn
    logits = lax.dot_general(rw_ref[...], xn, (((1,), (1,)), ((), ())),
                             precision=lax.Precision.HIGHEST, preferred_element_type=F32)
    scores = jax.nn.sigmoid(logits)
    (e1, e2), (w1, w2) = _route(scores, scores + rb_ref[...])
    eidx_ref[0:1, :] = e1
    eidx_ref[1:2, :] = e2
    gw_ref[0:1, :] = w1
    gw_ref[1:2, :] = w2


def _oproj_ln_route(o, wo, x, ln_g, ln_b, router_wt, router_b, *, alpha):
    t, d = x.shape
    tm = min(ROW_TM, t)
    assert t % tm == 0
    row = pl.BlockSpec((tm, d), lambda m: (m, 0))
    full = lambda a: pl.BlockSpec(a.shape, lambda m: (0,) * a.ndim)
    sel = pl.BlockSpec((TOP_K, tm), lambda m: (0, m))
    return pl.pallas_call(
        functools.partial(_oproj_ln_route_kernel, alpha=alpha),
        grid=(t // tm,),
        in_specs=[row, full(wo), row, full(ln_g), full(ln_b), full(router_wt), full(router_b)],
        out_specs=[row, sel, sel],
        out_shape=[jax.ShapeDtypeStruct((t, d), F32),
                   jax.ShapeDtypeStruct((TOP_K, t), jnp.int32),
                   jax.ShapeDtypeStruct((TOP_K, t), F32)],
        compiler_params=_params("parallel"),
        name="oproj_ln_route",
    )(o, wo, x, ln_g, ln_b, router_wt, router_b)


def _dispatch_plan(eidx, tm):
    t = eidx.shape[1]
    a = TOP_K * t
    flat_e = eidx.reshape(a)
    onehot = (flat_e[:, None] == jnp.arange(N_EXPERTS, dtype=jnp.int32)[None, :]).astype(jnp.int32)
    csum = jnp.cumsum(onehot, axis=0)
    counts = csum[-1]
    rank = jnp.sum(onehot * csum, axis=1) - 1
    padded = (counts + tm - 1) // tm * tm
    pad_end = jnp.cumsum(padded)
    pad_start = pad_end - padded
    start = jnp.cumsum(counts) - counts
    dest = jnp.sum(onehot * pad_start[None, :], axis=1) + rank
    n_rows = (a + tm - 1) // tm * tm + N_EXPERTS * tm
    n_blk = n_rows // tm
    blk_e = jnp.minimum(jnp.searchsorted(pad_end, jnp.arange(n_blk, dtype=jnp.int32) * tm,
                                         side='right'), N_EXPERTS - 1).astype(jnp.int32)
    n_used = (pad_end[-1] // tm).astype(jnp.int32).reshape(1)
    order = jnp.argsort(flat_e, stable=True).astype(jnp.int32)
    row = jnp.arange(n_rows, dtype=jnp.int32)
    row_e = jnp.repeat(blk_e, tm)
    j = row - pad_start[row_e]
    valid = (j >= 0) & (j < counts[row_e])
    src = order[jnp.clip(start[row_e] + j, 0, a - 1)] % t
    tok = jnp.where(valid, src, 0).astype(jnp.int32)
    return tok, dest.astype(jnp.int32), blk_e, n_used


def _moe_kernel(blk_e_ref, nused_ref, tok_ref, x_hbm, wg_ref, wu_ref, wd_ref, y_ref, xbuf, sem):
    del blk_e_ref
    i = pl.program_id(0)
    tm = y_ref.shape[0]
    n_used = nused_ref[0]
    slot = i % 2

    def start_rows(blk, sl):
        base = blk * tm
        for r in range(tm):
            pltpu.make_async_copy(x_hbm.at[pl.ds(tok_ref[base + r], 1)],
                                  xbuf.at[sl, pl.ds(r, 1)], sem.at[sl]).start()

    def wait_rows(sl):
        pltpu.make_async_copy(x_hbm.at[pl.ds(0, tm)], xbuf.at[sl], sem.at[sl]).wait()

    @pl.when(i == 0)
    def _():
        start_rows(0, 0)

    @pl.when(i < n_used)
    def _():
        start_rows(i + 1, 1 - slot)
        wait_rows(slot)
        x = xbuf[slot].astype(BF16)
        gate = jnp.dot(x, wg_ref[...], preferred_element_type=F32)
        up = jnp.dot(x, wu_ref[...], preferred_element_type=F32)
        h = (gate * jax.nn.sigmoid(gate) * up).astype(BF16)
        y_ref[...] = jnp.dot(h, wd_ref[...], preferred_element_type=F32)

    @pl.when(i == n_used)
    def _():
        wait_rows(slot)

    @pl.when(i >= n_used)
    def _():
        y_ref[...] = jnp.zeros_like(y_ref)


def _moe_experts(x, tok, blk_e, n_used, w_gate, w_up, w_down):
    d = x.shape[1]
    f = w_gate.shape[2]
    n_rows = tok.shape[0]
    tm = MOE_TM
    grid_spec = pltpu.PrefetchScalarGridSpec(
        num_scalar_prefetch=3,
        grid=(n_rows // tm,),
        in_specs=[
            pl.BlockSpec(memory_space=pl.ANY),
            pl.BlockSpec((None, d, f), lambda i, be, nu, tk: (be[i], 0, 0)),
            pl.BlockSpec((None, d, f), lambda i, be, nu, tk: (be[i], 0, 0)),
            pl.BlockSpec((None, f, d), lambda i, be, nu, tk: (be[i], 0, 0)),
        ],
        out_specs=pl.BlockSpec((tm, d), lambda i, be, nu, tk: (i, 0)),
        scratch_shapes=[pltpu.VMEM((2, tm, d), F32), pltpu.SemaphoreType.DMA((2,))],
    )
    return pl.pallas_call(
        _moe_kernel,
        grid_spec=grid_spec,
        out_shape=jax.ShapeDtypeStruct((n_rows, d), F32),
        compiler_params=_params("arbitrary"),
        name="moe_experts",
    )(blk_e, n_used, tok, x, w_gate, w_up, w_down)


def _combine_kernel(dest_ref, y_hbm, gw_ref, x1_ref, g_ref, b_ref, p_ref, wpg_ref, wpi_ref,
                    out_ref, ybuf, sem, *, alpha, n_tokens):
    i = pl.program_id(0)
    n = pl.num_programs(0)
    tm = out_ref.shape[0]
    slot = i % 2

    def start_rows(blk, sl):
        for k in range(TOP_K):
            base = k * n_tokens + blk * tm
            for r in range(tm):
                pltpu.make_async_copy(y_hbm.at[pl.ds(dest_ref[base + r], 1)],
                                      ybuf.at[sl, k, pl.ds(r, 1)], sem.at[sl]).start()

    def wait_rows(sl):
        for k in range(TOP_K):
            pltpu.make_async_copy(y_hbm.at[pl.ds(0, tm)], ybuf.at[sl, k], sem.at[sl]).wait()

    @pl.when(i == 0)
    def _():
        start_rows(0, 0)

    start_rows(jnp.minimum(i + 1, n - 1), 1 - slot)
    wait_rows(slot)
    gw = gw_ref[...]
    moe = ybuf[slot, 0] * gw[:, 0:1] + ybuf[slot, 1] * gw[:, 1:2]
    x2 = _layer_norm(alpha * x1_ref[...] + moe, g_ref[...], b_ref[...])
    gate = jax.nn.sigmoid(jnp.dot(x2.astype(BF16), wpg_ref[...], preferred_element_type=F32))
    emb = jnp.dot(p_ref[...].astype(BF16), wpi_ref[...], preferred_element_type=F32)
    out_ref[...] = x2 + gate * emb

    @pl.when(i == n - 1)
    def _():
        wait_rows(1 - slot)


def _combine_ln_gate(y, dest, gw_t, x1, ln_g, ln_b, p, w_pl_gate, w_pl_in, *, alpha):
    t, d = x1.shape
    tm = min(CMB_TM, t)
    assert t % tm == 0
    row = lambda width: pl.BlockSpec((tm, width), lambda m, ds: (m, 0))
    full = lambda a: pl.BlockSpec(a.shape, lambda m, ds: (0,) * a.ndim)
    grid_spec = pltpu.PrefetchScalarGridSpec(
        num_scalar_prefetch=1,
        grid=(t // tm,),
        in_specs=[
            pl.BlockSpec(memory_space=pl.ANY),
            row(TOP_K), row(d), full(ln_g), full(ln_b), row(p.shape[1]),
            full(w_pl_gate), full(w_pl_in),
        ],
        out_specs=row(d),
        scratch_shapes=[pltpu.VMEM((2, TOP_K, tm, d), F32), pltpu.SemaphoreType.DMA((2,))],
    )
    return pl.pallas_call(
        functools.partial(_combine_kernel, alpha=alpha, n_tokens=t),
        grid_spec=grid_spec,
        out_shape=jax.ShapeDtypeStruct((t, d), F32),
        compiler_params=_params("arbitrary"),
        name="combine_ln_gate",
    )(dest, y, gw_t, x1, ln_g, ln_b, p, w_pl_gate, w_pl_in)


def kernel(x_prompt, x_sample, p_prompt, p_sample, a_w_qkv, a_w_o, a_sink, b_w_qkv, b_w_o, b_lambda_q1, b_lambda_k1, b_lambda_q2, b_lambda_k2, b_subln_g, ln1_g, ln1_b, ln2_g, ln2_b, router_w, router_bias, exp_w_gate, exp_w_up, exp_w_down, pl_w_in, pl_w_gate):
    depth = ln1_g.shape[0]
    d = x_prompt.shape[-1]
    seq = x_prompt.shape[1]
    assert x_sample.shape[1] == seq
    n_prompt = x_prompt.shape[0]
    bsz = n_prompt + x_sample.shape[0]
    t = bsz * seq
    n_heads = d // HEAD_DIM
    alpha = (2.0 * depth) ** 0.25

    x = jnp.concatenate([x_prompt, x_sample], axis=0).reshape(t, d)
    p = jnp.concatenate([p_prompt, p_sample], axis=1).reshape(depth, t, p_prompt.shape[-1])

    a_tables = _rope_tables(seq, HEAD_DIM, HEAD_DIM ** -0.5 * math.log2(math.e))
    b_tables = _rope_tables(seq, HEAD_DIM // 2, (HEAD_DIM // 2) ** -0.5 * math.log2(math.e))
    router_wt = router_w.T.astype(F32)
    router_b = router_bias.astype(F32).reshape(N_EXPERTS, 1)
    row2d = lambda v: v.astype(F32).reshape(1, -1)

    for i in range(depth):
        j = i // N_MIXERS
        if i % N_MIXERS == 0:
            n_qk = (n_heads + A_KV_HEADS) * HEAD_DIM
            qk = _qkv_rope(x, a_w_qkv[j][:, :n_qk].astype(BF16), a_tables, seq=seq,
                           head_dim=HEAD_DIM, n_q_cols=n_heads * HEAD_DIM,
                           n_k_cols=A_KV_HEADS * HEAD_DIM)
            vt = _vt_proj(x, a_w_qkv[j][:, n_qk:].T.astype(BF16), seq=seq)
            o = _window_attention(qk, vt, a_sink[j].astype(F32), bsz=bsz, seq=seq,
                                  n_q_heads=n_heads)
            w_o = a_w_o[j]
        else:
            lambda_init = 0.8 - 0.6 * math.exp(-0.3 * i)
            n_qk = 2 * n_heads * HEAD_DIM
            qk = _qkv_rope(x, b_w_qkv[j][:, :n_qk].astype(BF16), b_tables, seq=seq,
                           head_dim=HEAD_DIM // 2, n_q_cols=n_heads * HEAD_DIM,
                           n_k_cols=n_heads * HEAD_DIM)
            vt = _vt_proj(x, b_w_qkv[j][:, n_qk:].T.astype(BF16), seq=seq)
            lam_params = jnp.stack([b_lambda_q1[j], b_lambda_k1[j], b_lambda_q2[j],
                                    b_lambda_k2[j]]).astype(F32)
            o = _diff_attention(qk, vt, lam_params, row2d(b_subln_g[j]), bsz=bsz, seq=seq,
                                n_heads=n_heads, lambda_init=lambda_init)
            w_o = b_w_o[j]
        x1, eidx, gw = _oproj_ln_route(o, w_o.astype(BF16), x, row2d(ln1_g[i]), row2d(ln1_b[i]),
                                       router_wt, router_b, alpha=alpha)
        tok, dest, blk_e, n_used = _dispatch_plan(eidx, MOE_TM)
        y = _moe_experts(x1, tok, blk_e, n_used, exp_w_gate[i].astype(BF16),
                         exp_w_up[i].astype(BF16), exp_w_down[i].astype(BF16))
        x = _combine_ln_gate(y, dest, gw.T, x1, row2d(ln2_g[i]), row2d(ln2_b[i]), p[i],
                             pl_w_gate[i].astype(BF16), pl_w_in[i].astype(BF16), alpha=alpha)

    x = x.reshape(bsz, seq, d)
    return (x[:n_prompt], x[n_prompt:])
```

```python
import functools
import math

import jax
import jax.numpy as jnp
from jax import lax
from jax.experimental import pallas as pl
from jax.experimental.pallas import tpu as pltpu

F32 = jnp.float32
BF16 = jnp.bfloat16

HEAD_DIM = 128
A_KV_HEADS = 4
WINDOW = 128
ROPE_THETA = 500000.0
ROPE_FRACTION = 4
N_EXPERTS = 16
N_GROUPS = 4
EXPERTS_PER_GROUP = N_EXPERTS // N_GROUPS
TOP_K = 2
N_MIXERS = 2
LN_EPS = 1e-5
NEG_INF = -1e30

LANES = 128
VMEM_LIMIT = 56 * 1024 * 1024

QKV_TM = 1024
QKV_TN = 512
QKV_SUB = 256
VT_TN = 1024
ATT_BLOCK = 128
DIFF_COLS = 256
DIFF_UNROLL = 4
DIFF_KEY_CHUNK = 1024
ROW_TM = 512
ROW_SUB = 256
MOE_TM = 256
CMB_TM = 256


def _params(*semantics):
    return pltpu.CompilerParams(dimension_semantics=semantics, vmem_limit_bytes=VMEM_LIMIT)


def _rope_tables(seq, head_dim, q_scale):
    rd = head_dim // ROPE_FRACTION
    half = rd // 2
    inv_freq = ROPE_THETA ** (-jnp.arange(half, dtype=F32) / half)
    ang = jnp.arange(seq, dtype=F32)[:, None] * inv_freq[None, :]
    cos, sin = jnp.cos(ang), jnp.sin(ang)
    ones = jnp.ones((seq, head_dim - rd), F32)
    zeros_half = jnp.zeros((seq, half), F32)
    zeros_rest = jnp.zeros((seq, head_dim - rd), F32)
    c = jnp.concatenate([cos, cos, ones], axis=1)
    s_up = jnp.concatenate([-sin, zeros_half, zeros_rest], axis=1)
    s_dn = jnp.concatenate([zeros_half, sin, zeros_rest], axis=1)
    reps = LANES // head_dim
    c, s_up, s_dn = (jnp.tile(t, (1, reps)) for t in (c, s_up, s_dn))
    ident = (jnp.ones_like(c), jnp.zeros_like(c), jnp.zeros_like(c))
    rope_k = (c, s_up, s_dn)
    rope_q = tuple(t * q_scale for t in rope_k)
    return tuple(jnp.stack([rope_q[j], rope_k[j], ident[j]]) for j in range(3))


def _qkv_rope_kernel(x_ref, w_ref, c_ref, su_ref, sd_ref, o_ref, xb_ref, *, half, n_rot_tiles):
    n = pl.program_id(1)
    tm, tn = o_ref.shape
    sub = min(QKV_SUB, tm)

    @pl.when(n == 0)
    def _():
        xb_ref[...] = x_ref[...].astype(BF16)

    def tile_loop(epilogue):
        def mm(r):
            return jnp.dot(xb_ref[r * sub:(r + 1) * sub, :], w_ref[...], preferred_element_type=F32)
        nxt = mm(0)
        for r in range(tm // sub):
            acc = nxt
            if r + 1 < tm // sub:
                nxt = mm(r + 1)
            o_ref[r * sub:(r + 1) * sub, :] = epilogue(acc, r).astype(o_ref.dtype)

    def rope(acc, r):
        rows = slice(r * sub, (r + 1) * sub)
        reps = tn // LANES
        c = jnp.tile(c_ref[0, rows, :], (1, reps))
        su = jnp.tile(su_ref[0, rows, :], (1, reps))
        sd = jnp.tile(sd_ref[0, rows, :], (1, reps))
        up = pltpu.roll(acc, tn - half, axis=1)
        dn = pltpu.roll(acc, half, axis=1)
        return acc * c + up * su + dn * sd

    @pl.when(n < n_rot_tiles)
    def _():
        tile_loop(rope)

    @pl.when(n >= n_rot_tiles)
    def _():
        tile_loop(lambda acc, r: acc)


def _qkv_rope(x, w, tables, *, seq, head_dim, n_q_cols, n_k_cols):
    t, d = x.shape
    n_cols = w.shape[1]
    tm = min(QKV_TM, seq)
    tn = QKV_TN
    q_tiles = n_q_cols // tn
    k_tiles = n_k_cols // tn
    half = head_dim // ROPE_FRACTION // 2
    seq_tiles = seq // tm
    assert seq % tm == 0 and n_q_cols % tn == 0 and n_k_cols % tn == 0 and n_cols % tn == 0

    def tab_map(m, n):
        kind = jnp.where(n < q_tiles, 0, jnp.where(n < q_tiles + k_tiles, 1, 2))
        return (kind, m % seq_tiles, 0)

    tab_spec = pl.BlockSpec((1, tm, LANES), tab_map)
    return pl.pallas_call(
        functools.partial(_qkv_rope_kernel, half=half, n_rot_tiles=q_tiles + k_tiles),
        grid=(t // tm, n_cols // tn),
        in_specs=[
            pl.BlockSpec((tm, d), lambda m, n: (m, 0)),
            pl.BlockSpec((d, tn), lambda m, n: (0, n)),
            tab_spec, tab_spec, tab_spec,
        ],
        out_specs=pl.BlockSpec((tm, tn), lambda m, n: (m, n)),
        out_shape=jax.ShapeDtypeStruct((t, n_cols), BF16),
        scratch_shapes=[pltpu.VMEM((tm, d), BF16)],
        compiler_params=_params("parallel", "arbitrary"),
        name="qkv_rope",
    )(x, w, *tables)


def _window_attn_kernel(sink_ref, q_ref, kp_ref, kc_ref, kn_ref, vp_ref, vc_ref, vn_ref, o_ref,
                        *, n_q_heads):
    n = pl.program_id(1)
    nb = pl.num_programs(1)
    blk = ATT_BLOCK
    group = n_q_heads // A_KV_HEADS
    c = lax.broadcasted_iota(jnp.int32, (3 * blk, group * blk), 0)
    r = lax.broadcasted_iota(jnp.int32, (3 * blk, group * blk), 1) % blk
    rel = c - r
    valid = (rel >= 0) & (rel <= 2 * WINDOW)
    valid = valid & ((c >= blk) | (n > 0)) & ((c < 2 * blk) | (n < nb - 1))
    log2e = math.log2(math.e)

    def scores(h):
        cols = slice(h * HEAD_DIM, (h + 1) * HEAD_DIM)
        kh = jnp.concatenate([kp_ref[:, cols], kc_ref[:, cols], kn_ref[:, cols]], axis=0)
        qg = jnp.concatenate([q_ref[:, (h * group + g) * HEAD_DIM:(h * group + g + 1) * HEAD_DIM]
                              for g in range(group)], axis=0)
        st = lax.dot_general(kh, qg, (((1,), (1,)), ((), ())), preferred_element_type=F32)
        return jnp.where(valid, st, NEG_INF)

    nxt = scores(0)
    for h in range(A_KV_HEADS):
        st = nxt
        if h + 1 < A_KV_HEADS:
            nxt = scores(h + 1)
        rows = slice(h * HEAD_DIM, (h + 1) * HEAD_DIM)
        vth = jnp.concatenate([vp_ref[rows, :], vc_ref[rows, :], vn_ref[rows, :]], axis=1)
        sink = jnp.concatenate([jnp.full((1, blk), sink_ref[h * group + g] * log2e, F32)
                                for g in range(group)], axis=1)
        m = jnp.maximum(jnp.max(st, axis=0, keepdims=True), sink)
        e = jnp.exp2(st - m)
        den = jnp.sum(e, axis=0, keepdims=True) + jnp.exp2(sink - m)
        ot = jnp.dot(vth, e.astype(BF16), preferred_element_type=F32) * (1.0 / den)
        for g in range(group):
            head = h * group + g
            o_ref[:, head * HEAD_DIM:(head + 1) * HEAD_DIM] = (
                ot[:, g * blk:(g + 1) * blk].T.astype(o_ref.dtype))


def _window_attention(qk, vt, sink, *, bsz, seq, n_q_heads):
    t = qk.shape[0]
    blk = ATT_BLOCK
    nb = seq // blk
    dq = n_q_heads * HEAD_DIM
    dk = A_KV_HEADS * HEAD_DIM
    k_col = dq // dk

    def shifted(n, shift):
        return jnp.clip(n + shift, 0, nb - 1)

    k_spec = lambda s: pl.BlockSpec((blk, dk), lambda b, n: (b * nb + shifted(n, s), k_col))
    v_spec = lambda s: pl.BlockSpec((dk, blk), lambda b, n: (0, b * nb + shifted(n, s)))
    return pl.pallas_call(
        functools.partial(_window_attn_kernel, n_q_heads=n_q_heads),
        grid=(bsz, nb),
        in_specs=[
            pl.BlockSpec(memory_space=pltpu.SMEM),
            pl.BlockSpec((blk, dq), lambda b, n: (b * nb + n, 0)),
            k_spec(-1), k_spec(0), k_spec(1), v_spec(-1), v_spec(0), v_spec(1),
        ],
        out_specs=pl.BlockSpec((blk, dq), lambda b, n: (b * nb + n, 0)),
        out_shape=jax.ShapeDtypeStruct((t, dq), BF16),
        compiler_params=_params("parallel", "arbitrary"),
        name="window_attention",
    )(sink, qk, qk, qk, qk, vt, vt, vt)


def _vt_proj_kernel(x_ref, wt_ref, o_ref, xb_ref):
    @pl.when(pl.program_id(1) == 0)
    def _():
        xb_ref[...] = x_ref[...].astype(BF16)

    o_ref[...] = lax.dot_general(wt_ref[...], xb_ref[...], (((1,), (1,)), ((), ())),
                                 preferred_element_type=F32).astype(o_ref.dtype)


def _vt_proj(x, w_t, *, seq):
    t, d = x.shape
    n_out = w_t.shape[0]
    tm = min(QKV_TM, seq)
    tn = min(VT_TN, n_out)
    assert t % tm == 0 and n_out % tn == 0
    return pl.pallas_call(
        _vt_proj_kernel,
        grid=(t // tm, n_out // tn),
        in_specs=[pl.BlockSpec((tm, d), lambda m, n: (m, 0)),
                  pl.BlockSpec((tn, d), lambda m, n: (n, 0))],
        out_specs=pl.BlockSpec((tn, tm), lambda m, n: (n, m)),
        out_shape=jax.ShapeDtypeStruct((n_out, t), BF16),
        scratch_shapes=[pltpu.VMEM((tm, d), BF16)],
        compiler_params=_params("parallel", "arbitrary"),
        name="vt_proj",
    )(x, w_t)


def _diff_attn_kernel(q_ref, k_ref, vt_ref, lam_ref, g_ref, o_ref, qq_ref, st_ref, ot_ref,
                      *, lambda_init):
    seq, dv = q_ref.shape
    cols = DIFF_COLS
    n_tiles = 2 * seq // cols
    q = q_ref[...]
    lane = lax.broadcasted_iota(jnp.int32, q.shape, 1)
    zero = jnp.zeros_like(q)
    qq_ref[0:seq, :] = jnp.where(lane < dv // 2, q, zero)
    qq_ref[seq:2 * seq, :] = jnp.where(lane >= dv // 2, q, zero)

    chunk = min(DIFF_KEY_CHUNK, seq)
    n_chunks = seq // chunk

    def scores(c, slot, j):
        qc = qq_ref[pl.ds(pl.multiple_of(c * cols, cols), cols), :]
        st = lax.dot_general(k_ref[j * chunk:(j + 1) * chunk, :], qc, (((1,), (1,)), ((), ())),
                             preferred_element_type=F32)
        st_ref[slot, j * chunk:(j + 1) * chunk, :] = st
        return jnp.max(st, axis=0, keepdims=True)

    def step(c, slot, m):
        c_next = jnp.minimum(c + 1, n_tiles - 1)
        m_next, l, acc = None, None, None
        for j in range(n_chunks):
            mj = scores(c_next, 1 - slot, j)
            m_next = mj if m_next is None else jnp.maximum(m_next, mj)
            e = jnp.exp2(st_ref[slot, j * chunk:(j + 1) * chunk, :] - m)
            lj = jnp.sum(e, axis=0, keepdims=True)
            pv = jnp.dot(vt_ref[:, j * chunk:(j + 1) * chunk], e.astype(BF16),
                         preferred_element_type=F32)
            l = lj if l is None else l + lj
            acc = pv if acc is None else acc + pv
        ot_ref[c] = acc * (1.0 / l)
        return m_next

    def group(i, m):
        c0 = DIFF_UNROLL * i
        for u in range(DIFF_UNROLL):
            m = step(c0 + u, u % 2, m)
        return m

    m0 = functools.reduce(jnp.maximum, [scores(0, 0, j) for j in range(n_chunks)])
    lax.fori_loop(0, n_tiles // DIFF_UNROLL, group, m0)

    lp = lam_ref[...]
    lam = (jnp.exp(jnp.sum(lp[0:1] * lp[1:2], axis=-1, keepdims=True))
           - jnp.exp(jnp.sum(lp[2:3] * lp[3:4], axis=-1, keepdims=True)) + lambda_init)
    half = n_tiles // 2
    for c in range(half):
        a = (ot_ref[c] - lam * ot_ref[half + c]).T
        a = a * lax.rsqrt(jnp.mean(a * a, axis=-1, keepdims=True) + LN_EPS) * g_ref[...]
        o_ref[c * cols:(c + 1) * cols, :] = (a * (1.0 - lambda_init)).astype(o_ref.dtype)


def _diff_attention(qk, vt, lam_params, subln_g, *, bsz, seq, n_heads, lambda_init):
    t = qk.shape[0]
    dv = HEAD_DIM
    n_tiles = 2 * seq // DIFF_COLS
    assert seq % DIFF_COLS == 0 and n_tiles % DIFF_UNROLL == 0 and DIFF_UNROLL % 2 == 0
    return pl.pallas_call(
        functools.partial(_diff_attn_kernel, lambda_init=lambda_init),
        grid=(bsz, n_heads),
        in_specs=[
            pl.BlockSpec((seq, dv), lambda b, h: (b, h)),
            pl.BlockSpec((seq, dv), lambda b, h: (b, n_heads + h)),
            pl.BlockSpec((dv, seq), lambda b, h: (h, b)),
            pl.BlockSpec(lam_params.shape, lambda b, h: (0, 0)),
            pl.BlockSpec(subln_g.shape, lambda b, h: (0, 0)),
        ],
        out_specs=pl.BlockSpec((seq, dv), lambda b, h: (b, h)),
        out_shape=jax.ShapeDtypeStruct((t, n_heads * dv), BF16),
        scratch_shapes=[pltpu.VMEM((2 * seq, dv), BF16),
                        pltpu.VMEM((2, seq, DIFF_COLS), F32),
                        pltpu.VMEM((n_tiles, dv, DIFF_COLS), F32)],
        compiler_params=_params("parallel", "parallel"),
        name="diff_attention",
    )(qk, qk, vt, lam_params, subln_g)


def _layer_norm(y, g, b):
    yc = y - jnp.mean(y, axis=-1, keepdims=True)
    var = jnp.mean(yc * yc, axis=-1, keepdims=True)
    return yc * lax.rsqrt(var + LN_EPS) * g + b


def _route(scores, biased):
    sb = [biased[e:e + 1] for e in range(N_EXPERTS)]
    sc = [scores[e:e + 1] for e in range(N_EXPERTS)]
    epg = EXPERTS_PER_GROUP
    gscore = []
    for g in range(N_GROUPS):
        v = sb[g * epg:(g + 1) * epg]
        best = None
        for i in range(epg):
            for j in range(i + 1, epg):
                pair = v[i] + v[j]
                best = pair if best is None else jnp.maximum(best, pair)
        gscore.append(best)
    gbest = gscore[0]
    gsel = jnp.zeros_like(gbest, dtype=jnp.int32)
    for g in range(1, N_GROUPS):
        upd = gscore[g] > gbest
        gbest = jnp.where(upd, gscore[g], gbest)
        gsel = jnp.where(upd, g, gsel)

    def pick_group(rows, j):
        out = rows[j]
        for g in range(1, N_GROUPS):
            out = jnp.where(gsel == g, rows[g * epg + j], out)
        return out

    v = [pick_group(sb, j) for j in range(epg)]
    u = [pick_group(sc, j) for j in range(epg)]
    b1, i1 = v[0], jnp.zeros_like(gsel)
    for j in range(1, epg):
        upd = v[j] > b1
        b1 = jnp.where(upd, v[j], b1)
        i1 = jnp.where(upd, j, i1)
    b2 = jnp.full_like(b1, -jnp.inf)
    i2 = jnp.zeros_like(gsel)
    for j in range(epg):
        upd = (i1 != j) & (v[j] > b2)
        b2 = jnp.where(upd, v[j], b2)
        i2 = jnp.where(upd, j, i2)

    def pick_local(idx):
        out = u[0]
        for j in range(1, epg):
            out = jnp.where(idx == j, u[j], out)
        return out

    w1, w2 = pick_local(i1), pick_local(i2)
    wsum = w1 + w2
    return (gsel * epg + i1, gsel * epg + i2), (w1 / wsum, w2 / wsum)


def _oproj_ln_route_kernel(o_ref, wo_ref, x_ref, g_ref, b_ref, rw_ref, rb_ref,
                           x1_ref, eidx_ref, gw_ref, *, alpha):
    tm = x_ref.shape[0]
    sub = min(ROW_SUB, tm)

    def mm(r):
        return jnp.dot(o_ref[r * sub:(r + 1) * sub, :], wo_ref[...], preferred_element_type=F32)

    nxt = mm(0)
    for r in range(tm // sub):
        h = nxt
        if r + 1 < tm // sub:
            nxt = mm(r + 1)
        rows = slice(r * sub, (r + 1) * sub)
        xn = _layer_norm(alpha * x_ref[rows, :] + h, g_ref[...], b_ref[...])
        x1_ref[rows, :] = xn
        logits = lax.dot_general(rw_ref[...], xn.astype(BF16), (((1,), (1,)), ((), ())),
                                 preferred_element_type=F32)
        scores = jax.nn.sigmoid(logits)
        (e1, e2), (w1, w2) = _route(scores, scores + rb_ref[...])
        eidx_ref[0:1, rows] = e1
        eidx_ref[1:2, rows] = e2
        gw_ref[0:1, rows] = w1
        gw_ref[1:2, rows] = w2


def _oproj_ln_route(o, wo, x, ln_g, ln_b, router_wt, router_b, *, alpha):
    t, d = x.shape
    tm = min(ROW_TM, t)
    assert t % tm == 0
    row = pl.BlockSpec((tm, d), lambda m: (m, 0))
    full = lambda a: pl.BlockSpec(a.shape, lambda m: (0,) * a.ndim)
    sel = pl.BlockSpec((TOP_K, tm), lambda m: (0, m))
    return pl.pallas_call(
        functools.partial(_oproj_ln_route_kernel, alpha=alpha),
        grid=(t // tm,),
        in_specs=[row, full(wo), row, full(ln_g), full(ln_b), full(router_wt), full(router_b)],
        out_specs=[row, sel, sel],
        out_shape=[jax.ShapeDtypeStruct((t, d), F32),
                   jax.ShapeDtypeStruct((TOP_K, t), jnp.int32),
                   jax.ShapeDtypeStruct((TOP_K, t), F32)],
        compiler_params=_params("parallel"),
        name="oproj_ln_route",
    )(o, wo, x, ln_g, ln_b, router_wt, router_b)


def _dispatch_plan(eidx, tm):
    t = eidx.shape[1]
    a = TOP_K * t
    flat_e = eidx.reshape(a)
    onehot = (flat_e[:, None] == jnp.arange(N_EXPERTS, dtype=jnp.int32)[None, :]).astype(jnp.int32)
    csum = jnp.cumsum(onehot, axis=0)
    counts = csum[-1]
    rank = jnp.sum(onehot * csum, axis=1) - 1
    padded = (counts + tm - 1) // tm * tm
    pad_end = jnp.cumsum(padded)
    pad_start = pad_end - padded
    start = jnp.cumsum(counts) - counts
    dest = jnp.sum(onehot * pad_start[None, :], axis=1) + rank
    n_rows = (a + tm - 1) // tm * tm + N_EXPERTS * tm
    n_blk = n_rows // tm
    blk_e = jnp.minimum(jnp.searchsorted(pad_end, jnp.arange(n_blk, dtype=jnp.int32) * tm,
                                         side='right'), N_EXPERTS - 1).astype(jnp.int32)
    n_used = (pad_end[-1] // tm).astype(jnp.int32).reshape(1)
    order = jnp.argsort(flat_e, stable=True).astype(jnp.int32)
    row = jnp.arange(n_rows, dtype=jnp.int32)
    row_e = jnp.repeat(blk_e, tm)
    j = row - pad_start[row_e]
    valid = (j >= 0) & (j < counts[row_e])
    src = order[jnp.clip(start[row_e] + j, 0, a - 1)] % t
    tok = jnp.where(valid, src, 0).astype(jnp.int32)
    return tok, dest.astype(jnp.int32), blk_e, n_used


def _moe_kernel(blk_e_ref, nused_ref, tok_ref, x_hbm, wg_ref, wu_ref, wd_ref, y_ref, xbuf, sem):
    del blk_e_ref
    i = pl.program_id(0)
    tm = y_ref.shape[0]
    n_used = nused_ref[0]
    slot = i % 2

    def start_rows(blk, sl):
        base = blk * tm
        for r in range(tm):
            pltpu.make_async_copy(x_hbm.at[pl.ds(tok_ref[base + r], 1)],
                                  xbuf.at[sl, pl.ds(r, 1)], sem.at[sl]).start()

    def wait_rows(sl):
        pltpu.make_async_copy(x_hbm.at[pl.ds(0, tm)], xbuf.at[sl], sem.at[sl]).wait()

    @pl.when(i == 0)
    def _():
        start_rows(0, 0)

    @pl.when(i < n_used)
    def _():
        start_rows(i + 1, 1 - slot)
        wait_rows(slot)
        x = xbuf[slot].astype(BF16)
        gate = jnp.dot(x, wg_ref[...], preferred_element_type=F32)
        up = jnp.dot(x, wu_ref[...], preferred_element_type=F32)
        h = (gate * jax.nn.sigmoid(gate) * up).astype(BF16)
        y_ref[...] = jnp.dot(h, wd_ref[...], preferred_element_type=F32)

    @pl.when(i == n_used)
    def _():
        wait_rows(slot)

    @pl.when(i >= n_used)
    def _():
        y_ref[...] = jnp.zeros_like(y_ref)


def _moe_experts(x, tok, blk_e, n_used, w_gate, w_up, w_down):
    d = x.shape[1]
    f = w_gate.shape[2]
    n_rows = tok.shape[0]
    tm = MOE_TM
    grid_spec = pltpu.PrefetchScalarGridSpec(
        num_scalar_prefetch=3,
        grid=(n_rows // tm,),
        in_specs=[
            pl.BlockSpec(memory_space=pl.ANY),
            pl.BlockSpec((None, d, f), lambda i, be, nu, tk: (be[i], 0, 0)),
            pl.BlockSpec((None, d, f), lambda i, be, nu, tk: (be[i], 0, 0)),
            pl.BlockSpec((None, f, d), lambda i, be, nu, tk: (be[i], 0, 0)),
        ],
        out_specs=pl.BlockSpec((tm, d), lambda i, be, nu, tk: (i, 0)),
        scratch_shapes=[pltpu.VMEM((2, tm, d), F32), pltpu.SemaphoreType.DMA((2,))],
    )
    return pl.pallas_call(
        _moe_kernel,
        grid_spec=grid_spec,
        out_shape=jax.ShapeDtypeStruct((n_rows, d), F32),
        compiler_params=_params("arbitrary"),
        name="moe_experts",
    )(blk_e, n_used, tok, x, w_gate, w_up, w_down)


def _combine_kernel(dest_ref, y_hbm, gw_ref, x1_ref, g_ref, b_ref, p_ref, wpg_ref, wpi_ref,
                    out_ref, ybuf, sem, *, alpha, n_tokens):
    i = pl.program_id(0)
    n = pl.num_programs(0)
    tm = out_ref.shape[0]
    slot = i % 2

    def start_rows(blk, sl):
        for k in range(TOP_K):
            base = k * n_tokens + blk * tm
            for r in range(tm):
                pltpu.make_async_copy(y_hbm.at[pl.ds(dest_ref[base + r], 1)],
                                      ybuf.at[sl, k, pl.ds(r, 1)], sem.at[sl]).start()

    def wait_rows(sl):
        for k in range(TOP_K):
            pltpu.make_async_copy(y_hbm.at[pl.ds(0, tm)], ybuf.at[sl, k], sem.at[sl]).wait()

    @pl.when(i == 0)
    def _():
        start_rows(0, 0)

    start_rows(jnp.minimum(i + 1, n - 1), 1 - slot)
    wait_rows(slot)
    gw = gw_ref[...]
    moe = ybuf[slot, 0] * gw[:, 0:1] + ybuf[slot, 1] * gw[:, 1:2]
    x2 = _layer_norm(alpha * x1_ref[...] + moe, g_ref[...], b_ref[...])
    gate = jax.nn.sigmoid(jnp.dot(x2.astype(BF16), wpg_ref[...], preferred_element_type=F32))
    emb = jnp.dot(p_ref[...].astype(BF16), wpi_ref[...], preferred_element_type=F32)
    out_ref[...] = x2 + gate * emb

    @pl.when(i == n - 1)
    def _():
        wait_rows(1 - slot)


def _combine_ln_gate(y, dest, gw_t, x1, ln_g, ln_b, p, w_pl_gate, w_pl_in, *, alpha):
    t, d = x1.shape
    tm = min(CMB_TM, t)
    assert t % tm == 0
    row = lambda width: pl.BlockSpec((tm, width), lambda m, ds: (m, 0))
    full = lambda a: pl.BlockSpec(a.shape, lambda m, ds: (0,) * a.ndim)
    grid_spec = pltpu.PrefetchScalarGridSpec(
        num_scalar_prefetch=1,
        grid=(t // tm,),
        in_specs=[
            pl.BlockSpec(memory_space=pl.ANY),
            row(TOP_K), row(d), full(ln_g), full(ln_b), row(p.shape[1]),
            full(w_pl_gate), full(w_pl_in),
        ],
        out_specs=row(d),
        scratch_shapes=[pltpu.VMEM((2, TOP_K, tm, d), F32), pltpu.SemaphoreType.DMA((2,))],
    )
    return pl.pallas_call(
        functools.partial(_combine_kernel, alpha=alpha, n_tokens=t),
        grid_spec=grid_spec,
        out_shape=jax.ShapeDtypeStruct((t, d), F32),
        compiler_params=_params("arbitrary"),
        name="combine_ln_gate",
    )(dest, y, gw_t, x1, ln_g, ln_b, p, w_pl_gate, w_pl_in)


def kernel(x_prompt, x_sample, p_prompt, p_sample, a_w_qkv, a_w_o, a_sink, b_w_qkv, b_w_o, b_lambda_q1, b_lambda_k1, b_lambda_q2, b_lambda_k2, b_subln_g, ln1_g, ln1_b, ln2_g, ln2_b, router_w, router_bias, exp_w_gate, exp_w_up, exp_w_down, pl_w_in, pl_w_gate):
    depth = ln1_g.shape[0]
    d = x_prompt.shape[-1]
    seq = x_prompt.shape[1]
    assert x_sample.shape[1] == seq
    n_prompt = x_prompt.shape[0]
    bsz = n_prompt + x_sample.shape[0]
    t = bsz * seq
    n_heads = d // HEAD_DIM
    alpha = (2.0 * depth) ** 0.25

    x = jnp.concatenate([x_prompt, x_sample], axis=0).reshape(t, d)
    p = jnp.concatenate([p_prompt, p_sample], axis=1).reshape(depth, t, p_prompt.shape[-1])

    a_tables = _rope_tables(seq, HEAD_DIM, HEAD_DIM ** -0.5 * math.log2(math.e))
    b_tables = _rope_tables(seq, HEAD_DIM // 2, (HEAD_DIM // 2) ** -0.5 * math.log2(math.e))
    router_wt = router_w.T.astype(BF16)
    router_b = router_bias.astype(F32).reshape(N_EXPERTS, 1)
    row2d = lambda v: v.astype(F32).reshape(1, -1)

    for i in range(depth):
        j = i // N_MIXERS
        if i % N_MIXERS == 0:
            n_qk = (n_heads + A_KV_HEADS) * HEAD_DIM
            qk = _qkv_rope(x, a_w_qkv[j][:, :n_qk].astype(BF16), a_tables, seq=seq,
                           head_dim=HEAD_DIM, n_q_cols=n_heads * HEAD_DIM,
                           n_k_cols=A_KV_HEADS * HEAD_DIM)
            vt = _vt_proj(x, a_w_qkv[j][:, n_qk:].T.astype(BF16), seq=seq)
            o = _window_attention(qk, vt, a_sink[j].astype(F32), bsz=bsz, seq=seq,
                                  n_q_heads=n_heads)
            w_o = a_w_o[j]
        else:
            lambda_init = 0.8 - 0.6 * math.exp(-0.3 * i)
            n_qk = 2 * n_heads * HEAD_DIM
            qk = _qkv_rope(x, b_w_qkv[j][:, :n_qk].astype(BF16), b_tables, seq=seq,
                           head_dim=HEAD_DIM // 2, n_q_cols=n_heads * HEAD_DIM,
                           n_k_cols=n_heads * HEAD_DIM)
            vt = _vt_proj(x, b_w_qkv[j][:, n_qk:].T.astype(BF16), seq=seq)
            lam_params = jnp.stack([b_lambda_q1[j], b_lambda_k1[j], b_lambda_q2[j],
                                    b_lambda_k2[j]]).astype(F32)
            o = _diff_attention(qk, vt, lam_params, row2d(b_subln_g[j]), bsz=bsz, seq=seq,
                                n_heads=n_heads, lambda_init=lambda_init)
            w_o = b_w_o[j]
        x1, eidx, gw = _oproj_ln_route(o, w_o.astype(BF16), x, row2d(ln1_g[i]), row2d(ln1_b[i]),
                                       router_wt, router_b, alpha=alpha)
        tok, dest, blk_e, n_used = _dispatch_plan(eidx, MOE_TM)
        y = _moe_experts(x1, tok, blk_e, n_used, exp_w_gate[i].astype(BF16),
                         exp_w_up[i].astype(BF16), exp_w_down[i].astype(BF16))
        x = _combine_ln_gate(y, dest, gw.T, x1, row2d(ln2_g[i]), row2d(ln2_b[i]), p[i],
                             pl_w_gate[i].astype(BF16), pl_w_in[i].astype(BF16), alpha=alpha)

    x = x.reshape(bsz, seq, d)
    return (x[:n_prompt], x[n_prompt:])
```

```python
import functools
import math

import jax
import jax.numpy as jnp
from jax import lax
from jax.experimental import pallas as pl
from jax.experimental.pallas import tpu as pltpu

F32 = jnp.float32
BF16 = jnp.bfloat16

HEAD_DIM = 128
A_KV_HEADS = 4
WINDOW = 128
ROPE_THETA = 500000.0
ROPE_FRACTION = 4
N_EXPERTS = 16
N_GROUPS = 4
EXPERTS_PER_GROUP = N_EXPERTS // N_GROUPS
TOP_K = 2
N_MIXERS = 2
LN_EPS = 1e-5
NEG_INF = -1e30

LANES = 128
VMEM_LIMIT = 56 * 1024 * 1024

QKV_TM = 1024
QKV_TN = 512
QKV_SUB = 256
VT_TN = 1024
ATT_BLOCK = 128
DIFF_COLS = 256
DIFF_UNROLL = 4
DIFF_KEY_CHUNK = 1024
ROW_TM = 512
ROW_SUB = 256
MOE_TM = 256
CMB_TM = 256


def _params(*semantics):
    return pltpu.CompilerParams(dimension_semantics=semantics, vmem_limit_bytes=VMEM_LIMIT)


def _pair_specs(tm, width, n_first):
    first = pl.BlockSpec((tm, width), lambda m, *_: (jnp.minimum(m, n_first - 1), 0))
    second = pl.BlockSpec((tm, width), lambda m, *_: (jnp.maximum(m - n_first, 0), 0))
    return first, second


def _n_first_tiles(xa, tm):
    assert xa.shape[0] % tm == 0
    return xa.shape[0] // tm


def _rope_tables(seq, head_dim, q_scale):
    rd = head_dim // ROPE_FRACTION
    half = rd // 2
    inv_freq = ROPE_THETA ** (-jnp.arange(half, dtype=F32) / half)
    ang = jnp.arange(seq, dtype=F32)[:, None] * inv_freq[None, :]
    cos, sin = jnp.cos(ang), jnp.sin(ang)
    ones = jnp.ones((seq, head_dim - rd), F32)
    zeros_half = jnp.zeros((seq, half), F32)
    zeros_rest = jnp.zeros((seq, head_dim - rd), F32)
    c = jnp.concatenate([cos, cos, ones], axis=1)
    s_up = jnp.concatenate([-sin, zeros_half, zeros_rest], axis=1)
    s_dn = jnp.concatenate([zeros_half, sin, zeros_rest], axis=1)
    reps = LANES // head_dim
    c, s_up, s_dn = (jnp.tile(t, (1, reps)) for t in (c, s_up, s_dn))
    ident = (jnp.ones_like(c), jnp.zeros_like(c), jnp.zeros_like(c))
    rope_k = (c, s_up, s_dn)
    rope_q = tuple(t * q_scale for t in rope_k)
    return tuple(jnp.stack([rope_q[j], rope_k[j], ident[j]]) for j in range(3))


def _qkv_rope_kernel(xa_ref, xb_ref, w_ref, c_ref, su_ref, sd_ref, o_ref, xbf_ref,
                     *, half, n_rot_tiles, n_first):
    m = pl.program_id(0)
    n = pl.program_id(1)
    tm, tn = o_ref.shape
    sub = min(QKV_SUB, tm)

    @pl.when((n == 0) & (m < n_first))
    def _():
        xbf_ref[...] = xa_ref[...].astype(BF16)

    @pl.when((n == 0) & (m >= n_first))
    def _():
        xbf_ref[...] = xb_ref[...].astype(BF16)

    def tile_loop(epilogue):
        def mm(r):
            return jnp.dot(xbf_ref[r * sub:(r + 1) * sub, :], w_ref[...], preferred_element_type=F32)
        nxt = mm(0)
        for r in range(tm // sub):
            acc = nxt
            if r + 1 < tm // sub:
                nxt = mm(r + 1)
            o_ref[r * sub:(r + 1) * sub, :] = epilogue(acc, r).astype(o_ref.dtype)

    def rope(acc, r):
        rows = slice(r * sub, (r + 1) * sub)
        reps = tn // LANES
        c = jnp.tile(c_ref[0, rows, :], (1, reps))
        su = jnp.tile(su_ref[0, rows, :], (1, reps))
        sd = jnp.tile(sd_ref[0, rows, :], (1, reps))
        up = pltpu.roll(acc, tn - half, axis=1)
        dn = pltpu.roll(acc, half, axis=1)
        return acc * c + up * su + dn * sd

    @pl.when(n < n_rot_tiles)
    def _():
        tile_loop(rope)

    @pl.when(n >= n_rot_tiles)
    def _():
        tile_loop(lambda acc, r: acc)


def _qkv_rope(xa, xb, w, tables, *, seq, head_dim, n_q_cols, n_k_cols):
    t, d = xa.shape[0] + xb.shape[0], xa.shape[1]
    n_cols = w.shape[1]
    tm = min(QKV_TM, seq)
    tn = QKV_TN
    q_tiles = n_q_cols // tn
    k_tiles = n_k_cols // tn
    half = head_dim // ROPE_FRACTION // 2
    seq_tiles = seq // tm
    assert seq % tm == 0 and n_q_cols % tn == 0 and n_k_cols % tn == 0 and n_cols % tn == 0

    def tab_map(m, n):
        kind = jnp.where(n < q_tiles, 0, jnp.where(n < q_tiles + k_tiles, 1, 2))
        return (kind, m % seq_tiles, 0)

    tab_spec = pl.BlockSpec((1, tm, LANES), tab_map)
    n_first = _n_first_tiles(xa, tm)
    return pl.pallas_call(
        functools.partial(_qkv_rope_kernel, half=half, n_rot_tiles=q_tiles + k_tiles,
                          n_first=n_first),
        grid=(t // tm, n_cols // tn),
        in_specs=[
            *_pair_specs(tm, d, n_first),
            pl.BlockSpec((d, tn), lambda m, n: (0, n)),
            tab_spec, tab_spec, tab_spec,
        ],
        out_specs=pl.BlockSpec((tm, tn), lambda m, n: (m, n)),
        out_shape=jax.ShapeDtypeStruct((t, n_cols), BF16),
        scratch_shapes=[pltpu.VMEM((tm, d), BF16)],
        compiler_params=_params("parallel", "arbitrary"),
        name="qkv_rope",
    )(xa, xb, w, *tables)


def _window_attn_kernel(sink_ref, q_ref, kp_ref, kc_ref, kn_ref, vp_ref, vc_ref, vn_ref, o_ref,
                        *, n_q_heads):
    n = pl.program_id(1)
    nb = pl.num_programs(1)
    blk = ATT_BLOCK
    group = n_q_heads // A_KV_HEADS
    c = lax.broadcasted_iota(jnp.int32, (3 * blk, group * blk), 0)
    r = lax.broadcasted_iota(jnp.int32, (3 * blk, group * blk), 1) % blk
    rel = c - r
    valid = (rel >= 0) & (rel <= 2 * WINDOW)
    valid = valid & ((c >= blk) | (n > 0)) & ((c < 2 * blk) | (n < nb - 1))
    log2e = math.log2(math.e)

    def scores(h):
        cols = slice(h * HEAD_DIM, (h + 1) * HEAD_DIM)
        kh = jnp.concatenate([kp_ref[:, cols], kc_ref[:, cols], kn_ref[:, cols]], axis=0)
        qg = jnp.concatenate([q_ref[:, (h * group + g) * HEAD_DIM:(h * group + g + 1) * HEAD_DIM]
                              for g in range(group)], axis=0)
        st = lax.dot_general(kh, qg, (((1,), (1,)), ((), ())), preferred_element_type=F32)
        return jnp.where(valid, st, NEG_INF)

    nxt = scores(0)
    for h in range(A_KV_HEADS):
        st = nxt
        if h + 1 < A_KV_HEADS:
            nxt = scores(h + 1)
        rows = slice(h * HEAD_DIM, (h + 1) * HEAD_DIM)
        vth = jnp.concatenate([vp_ref[rows, :], vc_ref[rows, :], vn_ref[rows, :]], axis=1)
        sink = jnp.concatenate([jnp.full((1, blk), sink_ref[h * group + g] * log2e, F32)
                                for g in range(group)], axis=1)
        m = jnp.maximum(jnp.max(st, axis=0, keepdims=True), sink)
        e = jnp.exp2(st - m)
        den = jnp.sum(e, axis=0, keepdims=True) + jnp.exp2(sink - m)
        ot = jnp.dot(vth, e.astype(BF16), preferred_element_type=F32) * (1.0 / den)
        for g in range(group):
            head = h * group + g
            o_ref[:, head * HEAD_DIM:(head + 1) * HEAD_DIM] = (
                ot[:, g * blk:(g + 1) * blk].T.astype(o_ref.dtype))


def _window_attention(qk, vt, sink, *, bsz, seq, n_q_heads):
    t = qk.shape[0]
    blk = ATT_BLOCK
    nb = seq // blk
    dq = n_q_heads * HEAD_DIM
    dk = A_KV_HEADS * HEAD_DIM
    k_col = dq // dk

    def shifted(n, shift):
        return jnp.clip(n + shift, 0, nb - 1)

    k_spec = lambda s: pl.BlockSpec((blk, dk), lambda b, n: (b * nb + shifted(n, s), k_col))
    v_spec = lambda s: pl.BlockSpec((dk, blk), lambda b, n: (0, b * nb + shifted(n, s)))
    return pl.pallas_call(
        functools.partial(_window_attn_kernel, n_q_heads=n_q_heads),
        grid=(bsz, nb),
        in_specs=[
            pl.BlockSpec(memory_space=pltpu.SMEM),
            pl.BlockSpec((blk, dq), lambda b, n: (b * nb + n, 0)),
            k_spec(-1), k_spec(0), k_spec(1), v_spec(-1), v_spec(0), v_spec(1),
        ],
        out_specs=pl.BlockSpec((blk, dq), lambda b, n: (b * nb + n, 0)),
        out_shape=jax.ShapeDtypeStruct((t, dq), BF16),
        compiler_params=_params("parallel", "arbitrary"),
        name="window_attention",
    )(sink, qk, qk, qk, qk, vt, vt, vt)


def _vt_proj_kernel(xa_ref, xb_ref, wt_ref, o_ref, xbf_ref, *, n_first):
    m = pl.program_id(0)
    n = pl.program_id(1)

    @pl.when((n == 0) & (m < n_first))
    def _():
        xbf_ref[...] = xa_ref[...].astype(BF16)

    @pl.when((n == 0) & (m >= n_first))
    def _():
        xbf_ref[...] = xb_ref[...].astype(BF16)

    o_ref[...] = lax.dot_general(wt_ref[...], xbf_ref[...], (((1,), (1,)), ((), ())),
                                 preferred_element_type=F32).astype(o_ref.dtype)


def _vt_proj(xa, xb, w_t, *, seq):
    t, d = xa.shape[0] + xb.shape[0], xa.shape[1]
    n_out = w_t.shape[0]
    tm = min(QKV_TM, seq)
    tn = min(VT_TN, n_out)
    assert t % tm == 0 and n_out % tn == 0
    n_first = _n_first_tiles(xa, tm)
    return pl.pallas_call(
        functools.partial(_vt_proj_kernel, n_first=n_first),
        grid=(t // tm, n_out // tn),
        in_specs=[*_pair_specs(tm, d, n_first),
                  pl.BlockSpec((tn, d), lambda m, n: (n, 0))],
        out_specs=pl.BlockSpec((tn, tm), lambda m, n: (n, m)),
        out_shape=jax.ShapeDtypeStruct((n_out, t), BF16),
        scratch_shapes=[pltpu.VMEM((tm, d), BF16)],
        compiler_params=_params("parallel", "arbitrary"),
        name="vt_proj",
    )(xa, xb, w_t)


def _diff_attn_kernel(q_ref, k_ref, vt_ref, lam_ref, g_ref, o_ref, qq_ref, st_ref, ot_ref,
                      *, lambda_init):
    seq, dv = q_ref.shape
    cols = DIFF_COLS
    n_tiles = 2 * seq // cols
    q = q_ref[...]
    lane = lax.broadcasted_iota(jnp.int32, q.shape, 1)
    zero = jnp.zeros_like(q)
    qq_ref[0:seq, :] = jnp.where(lane < dv // 2, q, zero)
    qq_ref[seq:2 * seq, :] = jnp.where(lane >= dv // 2, q, zero)

    chunk = min(DIFF_KEY_CHUNK, seq)
    n_chunks = seq // chunk

    def scores(c, slot, j):
        qc = qq_ref[pl.ds(pl.multiple_of(c * cols, cols), cols), :]
        st = lax.dot_general(k_ref[j * chunk:(j + 1) * chunk, :], qc, (((1,), (1,)), ((), ())),
                             preferred_element_type=F32)
        st_ref[slot, j * chunk:(j + 1) * chunk, :] = st
        return jnp.max(st, axis=0, keepdims=True)

    def step(c, slot, m):
        c_next = jnp.minimum(c + 1, n_tiles - 1)
        m_next, l, acc = None, None, None
        for j in range(n_chunks):
            mj = scores(c_next, 1 - slot, j)
            m_next = mj if m_next is None else jnp.maximum(m_next, mj)
            e = jnp.exp2(st_ref[slot, j * chunk:(j + 1) * chunk, :] - m)
            lj = jnp.sum(e, axis=0, keepdims=True)
            pv = jnp.dot(vt_ref[:, j * chunk:(j + 1) * chunk], e.astype(BF16),
                         preferred_element_type=F32)
            l = lj if l is None else l + lj
            acc = pv if acc is None else acc + pv
        ot_ref[c] = acc * (1.0 / l)
        return m_next

    def group(i, m):
        c0 = DIFF_UNROLL * i
        for u in range(DIFF_UNROLL):
            m = step(c0 + u, u % 2, m)
        return m

    m0 = functools.reduce(jnp.maximum, [scores(0, 0, j) for j in range(n_chunks)])
    lax.fori_loop(0, n_tiles // DIFF_UNROLL, group, m0)

    lp = lam_ref[...]
    lam = (jnp.exp(jnp.sum(lp[0:1] * lp[1:2], axis=-1, keepdims=True))
           - jnp.exp(jnp.sum(lp[2:3] * lp[3:4], axis=-1, keepdims=True)) + lambda_init)
    half = n_tiles // 2
    for c in range(half):
        a = (ot_ref[c] - lam * ot_ref[half + c]).T
        a = a * lax.rsqrt(jnp.mean(a * a, axis=-1, keepdims=True) + LN_EPS) * g_ref[...]
        o_ref[c * cols:(c + 1) * cols, :] = (a * (1.0 - lambda_init)).astype(o_ref.dtype)


def _diff_attention(qk, vt, lam_params, subln_g, *, bsz, seq, n_heads, lambda_init):
    t = qk.shape[0]
    dv = HEAD_DIM
    n_tiles = 2 * seq // DIFF_COLS
    assert seq % DIFF_COLS == 0 and n_tiles % DIFF_UNROLL == 0 and DIFF_UNROLL % 2 == 0
    return pl.pallas_call(
        functools.partial(_diff_attn_kernel, lambda_init=lambda_init),
        grid=(bsz, n_heads),
        in_specs=[
            pl.BlockSpec((seq, dv), lambda b, h: (b, h)),
            pl.BlockSpec((seq, dv), lambda b, h: (b, n_heads + h)),
            pl.BlockSpec((dv, seq), lambda b, h: (h, b)),
            pl.BlockSpec(lam_params.shape, lambda b, h: (0, 0)),
            pl.BlockSpec(subln_g.shape, lambda b, h: (0, 0)),
        ],
        out_specs=pl.BlockSpec((seq, dv), lambda b, h: (b, h)),
        out_shape=jax.ShapeDtypeStruct((t, n_heads * dv), BF16),
        scratch_shapes=[pltpu.VMEM((2 * seq, dv), BF16),
                        pltpu.VMEM((2, seq, DIFF_COLS), F32),
                        pltpu.VMEM((n_tiles, dv, DIFF_COLS), F32)],
        compiler_params=_params("parallel", "parallel"),
        name="diff_attention",
    )(qk, qk, vt, lam_params, subln_g)


def _layer_norm(y, g, b):
    yc = y - jnp.mean(y, axis=-1, keepdims=True)
    var = jnp.mean(yc * yc, axis=-1, keepdims=True)
    return yc * lax.rsqrt(var + LN_EPS) * g + b


def _route(scores, biased):
    sb = [biased[e:e + 1] for e in range(N_EXPERTS)]
    sc = [scores[e:e + 1] for e in range(N_EXPERTS)]
    epg = EXPERTS_PER_GROUP
    gscore = []
    for g in range(N_GROUPS):
        v = sb[g * epg:(g + 1) * epg]
        best = None
        for i in range(epg):
            for j in range(i + 1, epg):
                pair = v[i] + v[j]
                best = pair if best is None else jnp.maximum(best, pair)
        gscore.append(best)
    gbest = gscore[0]
    gsel = jnp.zeros_like(gbest, dtype=jnp.int32)
    for g in range(1, N_GROUPS):
        upd = gscore[g] > gbest
        gbest = jnp.where(upd, gscore[g], gbest)
        gsel = jnp.where(upd, g, gsel)

    def pick_group(rows, j):
        out = rows[j]
        for g in range(1, N_GROUPS):
            out = jnp.where(gsel == g, rows[g * epg + j], out)
        return out

    v = [pick_group(sb, j) for j in range(epg)]
    u = [pick_group(sc, j) for j in range(epg)]
    b1, i1 = v[0], jnp.zeros_like(gsel)
    for j in range(1, epg):
        upd = v[j] > b1
        b1 = jnp.where(upd, v[j], b1)
        i1 = jnp.where(upd, j, i1)
    b2 = jnp.full_like(b1, -jnp.inf)
    i2 = jnp.zeros_like(gsel)
    for j in range(epg):
        upd = (i1 != j) & (v[j] > b2)
        b2 = jnp.where(upd, v[j], b2)
        i2 = jnp.where(upd, j, i2)

    def pick_local(idx):
        out = u[0]
        for j in range(1, epg):
            out = jnp.where(idx == j, u[j], out)
        return out

    w1, w2 = pick_local(i1), pick_local(i2)
    wsum = w1 + w2
    return (gsel * epg + i1, gsel * epg + i2), (w1 / wsum, w2 / wsum)


def _oproj_ln_route_kernel(o_ref, wo_ref, xa_ref, xb_ref, g_ref, b_ref, rw_ref, rb_ref,
                           x1_ref, eidx_ref, gw_ref, *, alpha, n_first):
    tm = xa_ref.shape[0]
    from_first = pl.program_id(0) < n_first
    sub = min(ROW_SUB, tm)

    def mm(r):
        return jnp.dot(o_ref[r * sub:(r + 1) * sub, :], wo_ref[...], preferred_element_type=F32)

    nxt = mm(0)
    for r in range(tm // sub):
        h = nxt
        if r + 1 < tm // sub:
            nxt = mm(r + 1)
        rows = slice(r * sub, (r + 1) * sub)
        x = jnp.where(from_first, xa_ref[rows, :], xb_ref[rows, :])
        xn = _layer_norm(alpha * x + h, g_ref[...], b_ref[...])
        x1_ref[rows, :] = xn
        logits = lax.dot_general(rw_ref[...], xn.astype(BF16), (((1,), (1,)), ((), ())),
                                 preferred_element_type=F32)
        scores = jax.nn.sigmoid(logits)
        (e1, e2), (w1, w2) = _route(scores, scores + rb_ref[...])
        eidx_ref[0:1, rows] = e1
        eidx_ref[1:2, rows] = e2
        gw_ref[0:1, rows] = w1
        gw_ref[1:2, rows] = w2


def _oproj_ln_route(o, wo, xa, xb, ln_g, ln_b, router_wt, router_b, *, alpha):
    t, d = o.shape
    tm = min(ROW_TM, xa.shape[0])
    assert t % tm == 0
    n_first = _n_first_tiles(xa, tm)
    row = pl.BlockSpec((tm, d), lambda m: (m, 0))
    full = lambda a: pl.BlockSpec(a.shape, lambda m: (0,) * a.ndim)
    sel = pl.BlockSpec((TOP_K, tm), lambda m: (0, m))
    return pl.pallas_call(
        functools.partial(_oproj_ln_route_kernel, alpha=alpha, n_first=n_first),
        grid=(t // tm,),
        in_specs=[row, full(wo), *_pair_specs(tm, d, n_first), full(ln_g), full(ln_b),
                  full(router_wt), full(router_b)],
        out_specs=[row, sel, sel],
        out_shape=[jax.ShapeDtypeStruct((t, d), F32),
                   jax.ShapeDtypeStruct((TOP_K, t), jnp.int32),
                   jax.ShapeDtypeStruct((TOP_K, t), F32)],
        compiler_params=_params("parallel"),
        name="oproj_ln_route",
    )(o, wo, xa, xb, ln_g, ln_b, router_wt, router_b)


def _dispatch_plan(eidx, tm):
    t = eidx.shape[1]
    a = TOP_K * t
    flat_e = eidx.reshape(a)
    onehot = (flat_e[:, None] == jnp.arange(N_EXPERTS, dtype=jnp.int32)[None, :]).astype(jnp.int32)
    csum = jnp.cumsum(onehot, axis=0)
    counts = csum[-1]
    rank = jnp.sum(onehot * csum, axis=1) - 1
    padded = (counts + tm - 1) // tm * tm
    pad_end = jnp.cumsum(padded)
    pad_start = pad_end - padded
    start = jnp.cumsum(counts) - counts
    dest = jnp.sum(onehot * pad_start[None, :], axis=1) + rank
    n_rows = (a + tm - 1) // tm * tm + N_EXPERTS * tm
    n_blk = n_rows // tm
    blk_e = jnp.minimum(jnp.searchsorted(pad_end, jnp.arange(n_blk, dtype=jnp.int32) * tm,
                                         side='right'), N_EXPERTS - 1).astype(jnp.int32)
    n_used = (pad_end[-1] // tm).astype(jnp.int32).reshape(1)
    order = jnp.argsort(flat_e, stable=True).astype(jnp.int32)
    row = jnp.arange(n_rows, dtype=jnp.int32)
    row_e = jnp.repeat(blk_e, tm)
    j = row - pad_start[row_e]
    valid = (j >= 0) & (j < counts[row_e])
    src = order[jnp.clip(start[row_e] + j, 0, a - 1)] % t
    tok = jnp.where(valid, src, 0).astype(jnp.int32)
    return tok, dest.astype(jnp.int32), blk_e, n_used


def _moe_kernel(blk_e_ref, nused_ref, tok_ref, x_hbm, wg_ref, wu_ref, wd_ref, y_ref, xbuf, sem):
    del blk_e_ref
    i = pl.program_id(0)
    tm = y_ref.shape[0]
    n_used = nused_ref[0]
    slot = i % 2

    def start_rows(blk, sl):
        base = blk * tm
        for r in range(tm):
            pltpu.make_async_copy(x_hbm.at[pl.ds(tok_ref[base + r], 1)],
                                  xbuf.at[sl, pl.ds(r, 1)], sem.at[sl]).start()

    def wait_rows(sl):
        pltpu.make_async_copy(x_hbm.at[pl.ds(0, tm)], xbuf.at[sl], sem.at[sl]).wait()

    @pl.when(i == 0)
    def _():
        start_rows(0, 0)

    @pl.when(i < n_used)
    def _():
        start_rows(i + 1, 1 - slot)
        wait_rows(slot)
        x = xbuf[slot].astype(BF16)
        gate = jnp.dot(x, wg_ref[...], preferred_element_type=F32)
        up = jnp.dot(x, wu_ref[...], preferred_element_type=F32)
        h = (gate * jax.nn.sigmoid(gate) * up).astype(BF16)
        y_ref[...] = jnp.dot(h, wd_ref[...], preferred_element_type=F32)

    @pl.when(i == n_used)
    def _():
        wait_rows(slot)

    @pl.when(i >= n_used)
    def _():
        y_ref[...] = jnp.zeros_like(y_ref)


def _moe_experts(x, tok, blk_e, n_used, w_gate, w_up, w_down, *, layer):
    d = x.shape[1]
    f = w_gate.shape[3]
    n_rows = tok.shape[0]
    tm = MOE_TM
    grid_spec = pltpu.PrefetchScalarGridSpec(
        num_scalar_prefetch=3,
        grid=(n_rows // tm,),
        in_specs=[
            pl.BlockSpec(memory_space=pl.ANY),
            pl.BlockSpec((None, None, d, f), lambda i, be, nu, tk: (layer, be[i], 0, 0)),
            pl.BlockSpec((None, None, d, f), lambda i, be, nu, tk: (layer, be[i], 0, 0)),
            pl.BlockSpec((None, None, f, d), lambda i, be, nu, tk: (layer, be[i], 0, 0)),
        ],
        out_specs=pl.BlockSpec((tm, d), lambda i, be, nu, tk: (i, 0)),
        scratch_shapes=[pltpu.VMEM((2, tm, d), F32), pltpu.SemaphoreType.DMA((2,))],
    )
    return pl.pallas_call(
        _moe_kernel,
        grid_spec=grid_spec,
        out_shape=jax.ShapeDtypeStruct((n_rows, d), F32),
        compiler_params=_params("arbitrary"),
        name="moe_experts",
    )(blk_e, n_used, tok, x, w_gate, w_up, w_down)


def _combine_kernel(dest_ref, y_hbm, gw_ref, x1_ref, g_ref, b_ref, pa_ref, pb_ref, wpg_ref, wpi_ref,
                    outa_ref, outb_ref, ybuf, sem, *, alpha, n_tokens, n_first):
    i = pl.program_id(0)
    n = pl.num_programs(0)
    tm = outa_ref.shape[0]
    slot = i % 2

    def start_rows(blk, sl):
        for k in range(TOP_K):
            base = k * n_tokens + blk * tm
            for r in range(tm):
                pltpu.make_async_copy(y_hbm.at[pl.ds(dest_ref[base + r], 1)],
                                      ybuf.at[sl, k, pl.ds(r, 1)], sem.at[sl]).start()

    def wait_rows(sl):
        for k in range(TOP_K):
            pltpu.make_async_copy(y_hbm.at[pl.ds(0, tm)], ybuf.at[sl, k], sem.at[sl]).wait()

    @pl.when(i == 0)
    def _():
        start_rows(0, 0)

    start_rows(jnp.minimum(i + 1, n - 1), 1 - slot)
    wait_rows(slot)
    gw = gw_ref[...]
    moe = ybuf[slot, 0] * gw[:, 0:1] + ybuf[slot, 1] * gw[:, 1:2]
    x2 = _layer_norm(alpha * x1_ref[...] + moe, g_ref[...], b_ref[...])
    gate = jax.nn.sigmoid(jnp.dot(x2.astype(BF16), wpg_ref[...], preferred_element_type=F32))
    p = jnp.where(i < n_first, pa_ref[...], pb_ref[...])
    emb = jnp.dot(p.astype(BF16), wpi_ref[...], preferred_element_type=F32)
    res = x2 + gate * emb

    @pl.when(i < n_first)
    def _():
        outa_ref[...] = res

    @pl.when(i >= n_first)
    def _():
        outb_ref[...] = res

    @pl.when(i == n - 1)
    def _():
        wait_rows(1 - slot)


def _combine_ln_gate(y, dest, gw_t, x1, ln_g, ln_b, pa, pb, w_pl_gate, w_pl_in, *, alpha):
    t, d = x1.shape
    tm = min(CMB_TM, pa.shape[0])
    assert t % tm == 0
    n_first = _n_first_tiles(pa, tm)
    row = lambda width: pl.BlockSpec((tm, width), lambda m, ds: (m, 0))
    full = lambda a: pl.BlockSpec(a.shape, lambda m, ds: (0,) * a.ndim)
    grid_spec = pltpu.PrefetchScalarGridSpec(
        num_scalar_prefetch=1,
        grid=(t // tm,),
        in_specs=[
            pl.BlockSpec(memory_space=pl.ANY),
            row(TOP_K), row(d), full(ln_g), full(ln_b), *_pair_specs(tm, pa.shape[1], n_first),
            full(w_pl_gate), full(w_pl_in),
        ],
        out_specs=list(_pair_specs(tm, d, n_first)),
        scratch_shapes=[pltpu.VMEM((2, TOP_K, tm, d), F32), pltpu.SemaphoreType.DMA((2,))],
    )
    return pl.pallas_call(
        functools.partial(_combine_kernel, alpha=alpha, n_tokens=t, n_first=n_first),
        grid_spec=grid_spec,
        out_shape=[jax.ShapeDtypeStruct((pa.shape[0], d), F32),
                   jax.ShapeDtypeStruct((pb.shape[0], d), F32)],
        compiler_params=_params("arbitrary"),
        name="combine_ln_gate",
    )(dest, y, gw_t, x1, ln_g, ln_b, pa, pb, w_pl_gate, w_pl_in)


def kernel(x_prompt, x_sample, p_prompt, p_sample, a_w_qkv, a_w_o, a_sink, b_w_qkv, b_w_o, b_lambda_q1, b_lambda_k1, b_lambda_q2, b_lambda_k2, b_subln_g, ln1_g, ln1_b, ln2_g, ln2_b, router_w, router_bias, exp_w_gate, exp_w_up, exp_w_down, pl_w_in, pl_w_gate):
    depth = ln1_g.shape[0]
    d = x_prompt.shape[-1]
    seq = x_prompt.shape[1]
    assert x_sample.shape[1] == seq
    n_prompt = x_prompt.shape[0]
    bsz = n_prompt + x_sample.shape[0]
    t = bsz * seq
    n_heads = d // HEAD_DIM
    alpha = (2.0 * depth) ** 0.25

    xa = x_prompt.reshape(n_prompt * seq, d)
    xb = x_sample.reshape(t - n_prompt * seq, d)
    pl_dim = p_prompt.shape[-1]
    w_gate, w_up, w_down = (w.astype(BF16) for w in (exp_w_gate, exp_w_up, exp_w_down))

    a_tables = _rope_tables(seq, HEAD_DIM, HEAD_DIM ** -0.5 * math.log2(math.e))
    b_tables = _rope_tables(seq, HEAD_DIM // 2, (HEAD_DIM // 2) ** -0.5 * math.log2(math.e))
    router_wt = router_w.T.astype(BF16)
    router_b = router_bias.astype(F32).reshape(N_EXPERTS, 1)
    row2d = lambda v: v.astype(F32).reshape(1, -1)

    for i in range(depth):
        j = i // N_MIXERS
        if i % N_MIXERS == 0:
            n_qk = (n_heads + A_KV_HEADS) * HEAD_DIM
            qk = _qkv_rope(xa, xb, a_w_qkv[j][:, :n_qk].astype(BF16), a_tables, seq=seq,
                           head_dim=HEAD_DIM, n_q_cols=n_heads * HEAD_DIM,
                           n_k_cols=A_KV_HEADS * HEAD_DIM)
            vt = _vt_proj(xa, xb, a_w_qkv[j][:, n_qk:].T.astype(BF16), seq=seq)
            o = _window_attention(qk, vt, a_sink[j].astype(F32), bsz=bsz, seq=seq,
                                  n_q_heads=n_heads)
            w_o = a_w_o[j]
        else:
            lambda_init = 0.8 - 0.6 * math.exp(-0.3 * i)
            n_qk = 2 * n_heads * HEAD_DIM
            qk = _qkv_rope(xa, xb, b_w_qkv[j][:, :n_qk].astype(BF16), b_tables, seq=seq,
                           head_dim=HEAD_DIM // 2, n_q_cols=n_heads * HEAD_DIM,
                           n_k_cols=n_heads * HEAD_DIM)
            vt = _vt_proj(xa, xb, b_w_qkv[j][:, n_qk:].T.astype(BF16), seq=seq)
            lam_params = jnp.stack([b_lambda_q1[j], b_lambda_k1[j], b_lambda_q2[j],
                                    b_lambda_k2[j]]).astype(F32)
            o = _diff_attention(qk, vt, lam_params, row2d(b_subln_g[j]), bsz=bsz, seq=seq,
                                n_heads=n_heads, lambda_init=lambda_init)
            w_o = b_w_o[j]
        x1, eidx, gw = _oproj_ln_route(o, w_o.astype(BF16), xa, xb, row2d(ln1_g[i]),
                                       row2d(ln1_b[i]), router_wt, router_b, alpha=alpha)
        tok, dest, blk_e, n_used = _dispatch_plan(eidx, MOE_TM)
        y = _moe_experts(x1, tok, blk_e, n_used, w_gate, w_up, w_down, layer=i)
        xa, xb = _combine_ln_gate(y, dest, gw.T, x1, row2d(ln2_g[i]), row2d(ln2_b[i]),
                                  p_prompt[i].reshape(-1, pl_dim), p_sample[i].reshape(-1, pl_dim),
                                  pl_w_gate[i].astype(BF16), pl_w_in[i].astype(BF16), alpha=alpha)

    return (xa.reshape(x_prompt.shape), xb.reshape(x_sample.shape))
```

```python
import functools
import math

import jax
import jax.numpy as jnp
from jax import lax
from jax.experimental import pallas as pl
from jax.experimental.pallas import tpu as pltpu

F32 = jnp.float32
BF16 = jnp.bfloat16

HEAD_DIM = 128
A_KV_HEADS = 4
WINDOW = 128
ROPE_THETA = 500000.0
ROPE_FRACTION = 4
N_EXPERTS = 16
N_GROUPS = 4
EXPERTS_PER_GROUP = N_EXPERTS // N_GROUPS
TOP_K = 2
N_MIXERS = 2
LN_EPS = 1e-5
NEG_INF = -1e30

LANES = 128
VMEM_LIMIT = 56 * 1024 * 1024

QKV_TM = 1024
QKV_TN = 512
QKV_SUB = 256
VT_TN = 1024
ATT_BLOCK = 128
DIFF_COLS = 256
DIFF_UNROLL = 8
DIFF_KEY_CHUNK = 1024
ROW_TM = 512
ROW_SUB = 256
MOE_TM = 256
CMB_TM = 256


def _params(*semantics):
    return pltpu.CompilerParams(dimension_semantics=semantics, vmem_limit_bytes=VMEM_LIMIT)


def _pair_specs(tm, width, n_first):
    first = pl.BlockSpec((tm, width), lambda m, *_: (jnp.minimum(m, n_first - 1), 0))
    second = pl.BlockSpec((tm, width), lambda m, *_: (jnp.maximum(m - n_first, 0), 0))
    return first, second


def _n_first_tiles(xa, tm):
    assert xa.shape[0] % tm == 0
    return xa.shape[0] // tm


def _rope_tables(seq, head_dim, q_scale):
    rd = head_dim // ROPE_FRACTION
    half = rd // 2
    inv_freq = ROPE_THETA ** (-jnp.arange(half, dtype=F32) / half)
    ang = jnp.arange(seq, dtype=F32)[:, None] * inv_freq[None, :]
    cos, sin = jnp.cos(ang), jnp.sin(ang)
    ones = jnp.ones((seq, head_dim - rd), F32)
    zeros_half = jnp.zeros((seq, half), F32)
    zeros_rest = jnp.zeros((seq, head_dim - rd), F32)
    c = jnp.concatenate([cos, cos, ones], axis=1)
    s_up = jnp.concatenate([-sin, zeros_half, zeros_rest], axis=1)
    s_dn = jnp.concatenate([zeros_half, sin, zeros_rest], axis=1)
    reps = LANES // head_dim
    c, s_up, s_dn = (jnp.tile(t, (1, reps)) for t in (c, s_up, s_dn))
    ident = (jnp.ones_like(c), jnp.zeros_like(c), jnp.zeros_like(c))
    rope_k = (c, s_up, s_dn)
    rope_q = tuple(t * q_scale for t in rope_k)
    return tuple(jnp.stack([rope_q[j], rope_k[j], ident[j]]) for j in range(3))


def _qkv_rope_kernel(xa_ref, xb_ref, w_ref, c_ref, su_ref, sd_ref, o_ref, xbf_ref,
                     *, half, n_rot_tiles, n_first):
    m = pl.program_id(0)
    n = pl.program_id(1)
    tm, tn = o_ref.shape
    sub = min(QKV_SUB, tm)

    @pl.when((n == 0) & (m < n_first))
    def _():
        xbf_ref[...] = xa_ref[...].astype(BF16)

    @pl.when((n == 0) & (m >= n_first))
    def _():
        xbf_ref[...] = xb_ref[...].astype(BF16)

    def tile_loop(epilogue):
        def mm(r):
            return jnp.dot(xbf_ref[r * sub:(r + 1) * sub, :], w_ref[...], preferred_element_type=F32)
        nxt = mm(0)
        for r in range(tm // sub):
            acc = nxt
            if r + 1 < tm // sub:
                nxt = mm(r + 1)
            o_ref[r * sub:(r + 1) * sub, :] = epilogue(acc, r).astype(o_ref.dtype)

    def rope(acc, r):
        rows = slice(r * sub, (r + 1) * sub)
        reps = tn // LANES
        c = jnp.tile(c_ref[0, rows, :], (1, reps))
        su = jnp.tile(su_ref[0, rows, :], (1, reps))
        sd = jnp.tile(sd_ref[0, rows, :], (1, reps))
        up = pltpu.roll(acc, tn - half, axis=1)
        dn = pltpu.roll(acc, half, axis=1)
        return acc * c + up * su + dn * sd

    @pl.when(n < n_rot_tiles)
    def _():
        tile_loop(rope)

    @pl.when(n >= n_rot_tiles)
    def _():
        tile_loop(lambda acc, r: acc)


def _qkv_rope(xa, xb, w, tables, *, seq, head_dim, n_q_cols, n_k_cols):
    t, d = xa.shape[0] + xb.shape[0], xa.shape[1]
    n_cols = w.shape[1]
    tm = min(QKV_TM, seq)
    tn = QKV_TN
    q_tiles = n_q_cols // tn
    k_tiles = n_k_cols // tn
    half = head_dim // ROPE_FRACTION // 2
    seq_tiles = seq // tm
    assert seq % tm == 0 and n_q_cols % tn == 0 and n_k_cols % tn == 0 and n_cols % tn == 0

    def tab_map(m, n):
        kind = jnp.where(n < q_tiles, 0, jnp.where(n < q_tiles + k_tiles, 1, 2))
        return (kind, m % seq_tiles, 0)

    tab_spec = pl.BlockSpec((1, tm, LANES), tab_map)
    n_first = _n_first_tiles(xa, tm)
    return pl.pallas_call(
        functools.partial(_qkv_rope_kernel, half=half, n_rot_tiles=q_tiles + k_tiles,
                          n_first=n_first),
        grid=(t // tm, n_cols // tn),
        in_specs=[
            *_pair_specs(tm, d, n_first),
            pl.BlockSpec((d, tn), lambda m, n: (0, n)),
            tab_spec, tab_spec, tab_spec,
        ],
        out_specs=pl.BlockSpec((tm, tn), lambda m, n: (m, n)),
        out_shape=jax.ShapeDtypeStruct((t, n_cols), BF16),
        scratch_shapes=[pltpu.VMEM((tm, d), BF16)],
        compiler_params=_params("parallel", "arbitrary"),
        name="qkv_rope",
    )(xa, xb, w, *tables)


def _window_attn_kernel(sink_ref, q_ref, kp_ref, kc_ref, kn_ref, vp_ref, vc_ref, vn_ref, o_ref,
                        *, n_q_heads):
    n = pl.program_id(1)
    nb = pl.num_programs(1)
    blk = ATT_BLOCK
    group = n_q_heads // A_KV_HEADS
    c = lax.broadcasted_iota(jnp.int32, (3 * blk, group * blk), 0)
    r = lax.broadcasted_iota(jnp.int32, (3 * blk, group * blk), 1) % blk
    rel = c - r
    valid = (rel >= 0) & (rel <= 2 * WINDOW)
    valid = valid & ((c >= blk) | (n > 0)) & ((c < 2 * blk) | (n < nb - 1))
    log2e = math.log2(math.e)

    def scores(h):
        cols = slice(h * HEAD_DIM, (h + 1) * HEAD_DIM)
        kh = jnp.concatenate([kp_ref[:, cols], kc_ref[:, cols], kn_ref[:, cols]], axis=0)
        qg = jnp.concatenate([q_ref[:, (h * group + g) * HEAD_DIM:(h * group + g + 1) * HEAD_DIM]
                              for g in range(group)], axis=0)
        st = lax.dot_general(kh, qg, (((1,), (1,)), ((), ())), preferred_element_type=F32)
        return jnp.where(valid, st, NEG_INF)

    nxt = scores(0)
    for h in range(A_KV_HEADS):
        st = nxt
        if h + 1 < A_KV_HEADS:
            nxt = scores(h + 1)
        rows = slice(h * HEAD_DIM, (h + 1) * HEAD_DIM)
        vth = jnp.concatenate([vp_ref[rows, :], vc_ref[rows, :], vn_ref[rows, :]], axis=1)
        sink = jnp.concatenate([jnp.full((1, blk), sink_ref[h * group + g] * log2e, F32)
                                for g in range(group)], axis=1)
        m = jnp.maximum(jnp.max(st, axis=0, keepdims=True), sink)
        e = jnp.exp2(st - m)
        den = jnp.sum(e, axis=0, keepdims=True) + jnp.exp2(sink - m)
        ot = jnp.dot(vth, e.astype(BF16), preferred_element_type=F32) * (1.0 / den)
        for g in range(group):
            head = h * group + g
            o_ref[:, head * HEAD_DIM:(head + 1) * HEAD_DIM] = (
                ot[:, g * blk:(g + 1) * blk].T.astype(o_ref.dtype))


def _window_attention(qk, vt, sink, *, bsz, seq, n_q_heads):
    t = qk.shape[0]
    blk = ATT_BLOCK
    nb = seq // blk
    dq = n_q_heads * HEAD_DIM
    dk = A_KV_HEADS * HEAD_DIM
    k_col = dq // dk

    def shifted(n, shift):
        return jnp.clip(n + shift, 0, nb - 1)

    k_spec = lambda s: pl.BlockSpec((blk, dk), lambda b, n: (b * nb + shifted(n, s), k_col))
    v_spec = lambda s: pl.BlockSpec((dk, blk), lambda b, n: (0, b * nb + shifted(n, s)))
    return pl.pallas_call(
        functools.partial(_window_attn_kernel, n_q_heads=n_q_heads),
        grid=(bsz, nb),
        in_specs=[
            pl.BlockSpec(memory_space=pltpu.SMEM),
            pl.BlockSpec((blk, dq), lambda b, n: (b * nb + n, 0)),
            k_spec(-1), k_spec(0), k_spec(1), v_spec(-1), v_spec(0), v_spec(1),
        ],
        out_specs=pl.BlockSpec((blk, dq), lambda b, n: (b * nb + n, 0)),
        out_shape=jax.ShapeDtypeStruct((t, dq), BF16),
        compiler_params=_params("parallel", "arbitrary"),
        name="window_attention",
    )(sink, qk, qk, qk, qk, vt, vt, vt)


def _vt_proj_kernel(xa_ref, xb_ref, wt_ref, o_ref, xbf_ref, *, n_first):
    m = pl.program_id(0)
    n = pl.program_id(1)

    @pl.when((n == 0) & (m < n_first))
    def _():
        xbf_ref[...] = xa_ref[...].astype(BF16)

    @pl.when((n == 0) & (m >= n_first))
    def _():
        xbf_ref[...] = xb_ref[...].astype(BF16)

    o_ref[...] = lax.dot_general(wt_ref[...], xbf_ref[...], (((1,), (1,)), ((), ())),
                                 preferred_element_type=F32).astype(o_ref.dtype)


def _vt_proj(xa, xb, w_t, *, seq):
    t, d = xa.shape[0] + xb.shape[0], xa.shape[1]
    n_out = w_t.shape[0]
    tm = min(QKV_TM, seq)
    tn = min(VT_TN, n_out)
    assert t % tm == 0 and n_out % tn == 0
    n_first = _n_first_tiles(xa, tm)
    return pl.pallas_call(
        functools.partial(_vt_proj_kernel, n_first=n_first),
        grid=(t // tm, n_out // tn),
        in_specs=[*_pair_specs(tm, d, n_first),
                  pl.BlockSpec((tn, d), lambda m, n: (n, 0))],
        out_specs=pl.BlockSpec((tn, tm), lambda m, n: (n, m)),
        out_shape=jax.ShapeDtypeStruct((n_out, t), BF16),
        scratch_shapes=[pltpu.VMEM((tm, d), BF16)],
        compiler_params=_params("parallel", "arbitrary"),
        name="vt_proj",
    )(xa, xb, w_t)


def _diff_attn_kernel(q_ref, k_ref, vt_ref, lam_ref, g_ref, o_ref, qq_ref, st_ref, ot_ref,
                      *, lambda_init):
    seq, dv = q_ref.shape
    cols = DIFF_COLS
    n_tiles = 2 * seq // cols
    q = q_ref[...]
    lane = lax.broadcasted_iota(jnp.int32, q.shape, 1)
    zero = jnp.zeros_like(q)
    qq_ref[0:seq, :] = jnp.where(lane < dv // 2, q, zero)
    qq_ref[seq:2 * seq, :] = jnp.where(lane >= dv // 2, q, zero)

    chunk = min(DIFF_KEY_CHUNK, seq)
    n_chunks = seq // chunk

    def scores(c, slot, j):
        qc = qq_ref[pl.ds(pl.multiple_of(c * cols, cols), cols), :]
        st = lax.dot_general(k_ref[j * chunk:(j + 1) * chunk, :], qc, (((1,), (1,)), ((), ())),
                             preferred_element_type=F32)
        st_ref[slot, j * chunk:(j + 1) * chunk, :] = st
        return jnp.max(st, axis=0, keepdims=True)

    def step(c, slot, m):
        c_next = jnp.minimum(c + 1, n_tiles - 1)
        m_next, l, acc = None, None, None
        for j in range(n_chunks):
            mj = scores(c_next, 1 - slot, j)
            m_next = mj if m_next is None else jnp.maximum(m_next, mj)
            e = jnp.exp2(st_ref[slot, j * chunk:(j + 1) * chunk, :] - m)
            lj = jnp.sum(e, axis=0, keepdims=True)
            pv = jnp.dot(vt_ref[:, j * chunk:(j + 1) * chunk], e.astype(BF16),
                         preferred_element_type=F32)
            l = lj if l is None else l + lj
            acc = pv if acc is None else acc + pv
        ot_ref[c] = acc * (1.0 / l)
        return m_next

    def group(i, m):
        c0 = DIFF_UNROLL * i
        for u in range(DIFF_UNROLL):
            m = step(c0 + u, u % 2, m)
        return m

    m0 = functools.reduce(jnp.maximum, [scores(0, 0, j) for j in range(n_chunks)])
    lax.fori_loop(0, n_tiles // DIFF_UNROLL, group, m0)

    lp = lam_ref[...]
    lam = (jnp.exp(jnp.sum(lp[0:1] * lp[1:2], axis=-1, keepdims=True))
           - jnp.exp(jnp.sum(lp[2:3] * lp[3:4], axis=-1, keepdims=True)) + lambda_init)
    half = n_tiles // 2
    for c in range(half):
        a = (ot_ref[c] - lam * ot_ref[half + c]).T
        a = a * lax.rsqrt(jnp.mean(a * a, axis=-1, keepdims=True) + LN_EPS) * g_ref[...]
        o_ref[c * cols:(c + 1) * cols, :] = (a * (1.0 - lambda_init)).astype(o_ref.dtype)


def _diff_attention(qk, vt, lam_params, subln_g, *, bsz, seq, n_heads, lambda_init):
    t = qk.shape[0]
    dv = HEAD_DIM
    n_tiles = 2 * seq // DIFF_COLS
    assert seq % DIFF_COLS == 0 and n_tiles % DIFF_UNROLL == 0 and DIFF_UNROLL % 2 == 0
    return pl.pallas_call(
        functools.partial(_diff_attn_kernel, lambda_init=lambda_init),
        grid=(bsz, n_heads),
        in_specs=[
            pl.BlockSpec((seq, dv), lambda b, h: (b, h)),
            pl.BlockSpec((seq, dv), lambda b, h: (b, n_heads + h)),
            pl.BlockSpec((dv, seq), lambda b, h: (h, b)),
            pl.BlockSpec(lam_params.shape, lambda b, h: (0, 0)),
            pl.BlockSpec(subln_g.shape, lambda b, h: (0, 0)),
        ],
        out_specs=pl.BlockSpec((seq, dv), lambda b, h: (b, h)),
        out_shape=jax.ShapeDtypeStruct((t, n_heads * dv), BF16),
        scratch_shapes=[pltpu.VMEM((2 * seq, dv), BF16),
                        pltpu.VMEM((2, seq, DIFF_COLS), F32),
                        pltpu.VMEM((n_tiles, dv, DIFF_COLS), F32)],
        compiler_params=_params("parallel", "parallel"),
        name="diff_attention",
    )(qk, qk, vt, lam_params, subln_g)


def _layer_norm(y, g, b):
    yc = y - jnp.mean(y, axis=-1, keepdims=True)
    var = jnp.mean(yc * yc, axis=-1, keepdims=True)
    return yc * lax.rsqrt(var + LN_EPS) * g + b


def _route(scores, biased):
    sb = [biased[e:e + 1] for e in range(N_EXPERTS)]
    sc = [scores[e:e + 1] for e in range(N_EXPERTS)]
    epg = EXPERTS_PER_GROUP
    gscore = []
    for g in range(N_GROUPS):
        v = sb[g * epg:(g + 1) * epg]
        best = None
        for i in range(epg):
            for j in range(i + 1, epg):
                pair = v[i] + v[j]
                best = pair if best is None else jnp.maximum(best, pair)
        gscore.append(best)
    gbest = gscore[0]
    gsel = jnp.zeros_like(gbest, dtype=jnp.int32)
    for g in range(1, N_GROUPS):
        upd = gscore[g] > gbest
        gbest = jnp.where(upd, gscore[g], gbest)
        gsel = jnp.where(upd, g, gsel)

    def pick_group(rows, j):
        out = rows[j]
        for g in range(1, N_GROUPS):
            out = jnp.where(gsel == g, rows[g * epg + j], out)
        return out

    v = [pick_group(sb, j) for j in range(epg)]
    u = [pick_group(sc, j) for j in range(epg)]
    b1, i1 = v[0], jnp.zeros_like(gsel)
    for j in range(1, epg):
        upd = v[j] > b1
        b1 = jnp.where(upd, v[j], b1)
        i1 = jnp.where(upd, j, i1)
    b2 = jnp.full_like(b1, -jnp.inf)
    i2 = jnp.zeros_like(gsel)
    for j in range(epg):
        upd = (i1 != j) & (v[j] > b2)
        b2 = jnp.where(upd, v[j], b2)
        i2 = jnp.where(upd, j, i2)

    def pick_local(idx):
        out = u[0]
        for j in range(1, epg):
            out = jnp.where(idx == j, u[j], out)
        return out

    w1, w2 = pick_local(i1), pick_local(i2)
    wsum = w1 + w2
    return (gsel * epg + i1, gsel * epg + i2), (w1 / wsum, w2 / wsum)


def _oproj_ln_route_kernel(o_ref, wo_ref, xa_ref, xb_ref, g_ref, b_ref, rw_ref, rb_ref,
                           x1_ref, eidx_ref, gw_ref, *, alpha, n_first):
    tm = xa_ref.shape[0]
    from_first = pl.program_id(0) < n_first
    sub = min(ROW_SUB, tm)

    def mm(r):
        return jnp.dot(o_ref[r * sub:(r + 1) * sub, :], wo_ref[...], preferred_element_type=F32)

    nxt = mm(0)
    for r in range(tm // sub):
        h = nxt
        if r + 1 < tm // sub:
            nxt = mm(r + 1)
        rows = slice(r * sub, (r + 1) * sub)
        x = jnp.where(from_first, xa_ref[rows, :], xb_ref[rows, :])
        xn = _layer_norm(alpha * x + h, g_ref[...], b_ref[...])
        x1_ref[rows, :] = xn
        logits = lax.dot_general(rw_ref[...], xn.astype(BF16), (((1,), (1,)), ((), ())),
                                 preferred_element_type=F32)
        scores = jax.nn.sigmoid(logits)
        (e1, e2), (w1, w2) = _route(scores, scores + rb_ref[...])
        eidx_ref[0:1, rows] = e1
        eidx_ref[1:2, rows] = e2
        gw_ref[0:1, rows] = w1
        gw_ref[1:2, rows] = w2


def _oproj_ln_route(o, wo, xa, xb, ln_g, ln_b, router_wt, router_b, *, alpha):
    t, d = o.shape
    tm = min(ROW_TM, xa.shape[0])
    assert t % tm == 0
    n_first = _n_first_tiles(xa, tm)
    row = pl.BlockSpec((tm, d), lambda m: (m, 0))
    full = lambda a: pl.BlockSpec(a.shape, lambda m: (0,) * a.ndim)
    sel = pl.BlockSpec((TOP_K, tm), lambda m: (0, m))
    return pl.pallas_call(
        functools.partial(_oproj_ln_route_kernel, alpha=alpha, n_first=n_first),
        grid=(t // tm,),
        in_specs=[row, full(wo), *_pair_specs(tm, d, n_first), full(ln_g), full(ln_b),
                  full(router_wt), full(router_b)],
        out_specs=[row, sel, sel],
        out_shape=[jax.ShapeDtypeStruct((t, d), F32),
                   jax.ShapeDtypeStruct((TOP_K, t), jnp.int32),
                   jax.ShapeDtypeStruct((TOP_K, t), F32)],
        compiler_params=_params("parallel"),
        name="oproj_ln_route",
    )(o, wo, xa, xb, ln_g, ln_b, router_wt, router_b)


def _dispatch_plan(eidx, tm):
    t = eidx.shape[1]
    a = TOP_K * t
    flat_e = eidx.reshape(a)
    onehot = (flat_e[:, None] == jnp.arange(N_EXPERTS, dtype=jnp.int32)[None, :]).astype(jnp.int32)
    csum = jnp.cumsum(onehot, axis=0)
    counts = csum[-1]
    rank = jnp.sum(onehot * csum, axis=1) - 1
    padded = (counts + tm - 1) // tm * tm
    pad_end = jnp.cumsum(padded)
    pad_start = pad_end - padded
    start = jnp.cumsum(counts) - counts
    dest = jnp.sum(onehot * pad_start[None, :], axis=1) + rank
    n_rows = (a + tm - 1) // tm * tm + N_EXPERTS * tm
    n_blk = n_rows // tm
    blk_start = jnp.arange(n_blk, dtype=jnp.int32) * tm
    blk_e = jnp.minimum(jnp.sum((pad_end[None, :] <= blk_start[:, None]).astype(jnp.int32), axis=1),
                        N_EXPERTS - 1)
    n_used = (pad_end[-1] // tm).astype(jnp.int32).reshape(1)
    order = jnp.sort(flat_e * a + jnp.arange(a, dtype=jnp.int32)) % a
    j = (blk_start - pad_start[blk_e])[:, None] + jnp.arange(tm, dtype=jnp.int32)[None, :]
    valid = (j >= 0) & (j < counts[blk_e][:, None])
    src = order[jnp.clip(start[blk_e][:, None] + j, 0, a - 1).reshape(n_rows)] % t
    tok = jnp.where(valid.reshape(n_rows), src, 0).astype(jnp.int32)
    return tok, dest.astype(jnp.int32), blk_e, n_used


def _moe_kernel(blk_e_ref, nused_ref, tok_ref, x_hbm, wg_ref, wu_ref, wd_ref, y_ref, xbuf, sem):
    del blk_e_ref
    i = pl.program_id(0)
    tm = y_ref.shape[0]
    n_used = nused_ref[0]
    slot = i % 2

    def start_rows(blk, sl):
        base = blk * tm
        for r in range(tm):
            pltpu.make_async_copy(x_hbm.at[pl.ds(tok_ref[base + r], 1)],
                                  xbuf.at[sl, pl.ds(r, 1)], sem.at[sl]).start()

    def wait_rows(sl):
        pltpu.make_async_copy(x_hbm.at[pl.ds(0, tm)], xbuf.at[sl], sem.at[sl]).wait()

    @pl.when(i == 0)
    def _():
        start_rows(0, 0)

    @pl.when(i < n_used)
    def _():
        start_rows(i + 1, 1 - slot)
        wait_rows(slot)
        x = xbuf[slot].astype(BF16)
        gate = jnp.dot(x, wg_ref[...], preferred_element_type=F32)
        up = jnp.dot(x, wu_ref[...], preferred_element_type=F32)
        h = (gate * jax.nn.sigmoid(gate) * up).astype(BF16)
        y_ref[...] = jnp.dot(h, wd_ref[...], preferred_element_type=F32)

    @pl.when(i == n_used)
    def _():
        wait_rows(slot)

    @pl.when(i >= n_used)
    def _():
        y_ref[...] = jnp.zeros_like(y_ref)


def _moe_experts(x, tok, blk_e, n_used, w_gate, w_up, w_down, *, layer):
    d = x.shape[1]
    f = w_gate.shape[3]
    n_rows = tok.shape[0]
    tm = MOE_TM
    grid_spec = pltpu.PrefetchScalarGridSpec(
        num_scalar_prefetch=3,
        grid=(n_rows // tm,),
        in_specs=[
            pl.BlockSpec(memory_space=pl.ANY),
            pl.BlockSpec((None, None, d, f), lambda i, be, nu, tk: (layer, be[i], 0, 0)),
            pl.BlockSpec((None, None, d, f), lambda i, be, nu, tk: (layer, be[i], 0, 0)),
            pl.BlockSpec((None, None, f, d), lambda i, be, nu, tk: (layer, be[i], 0, 0)),
        ],
        out_specs=pl.BlockSpec((tm, d), lambda i, be, nu, tk: (i, 0)),
        scratch_shapes=[pltpu.VMEM((2, tm, d), F32), pltpu.SemaphoreType.DMA((2,))],
    )
    return pl.pallas_call(
        _moe_kernel,
        grid_spec=grid_spec,
        out_shape=jax.ShapeDtypeStruct((n_rows, d), F32),
        compiler_params=_params("arbitrary"),
        name="moe_experts",
    )(blk_e, n_used, tok, x, w_gate, w_up, w_down)


def _combine_kernel(dest_ref, y_hbm, gw_ref, x1_ref, g_ref, b_ref, pa_ref, pb_ref, wpg_ref, wpi_ref,
                    outa_ref, outb_ref, ybuf, sem, *, alpha, n_tokens, n_first):
    i = pl.program_id(0)
    n = pl.num_programs(0)
    tm = outa_ref.shape[0]
    slot = i % 2

    def start_rows(blk, sl):
        for k in range(TOP_K):
            base = k * n_tokens + blk * tm
            for r in range(tm):
                pltpu.make_async_copy(y_hbm.at[pl.ds(dest_ref[base + r], 1)],
                                      ybuf.at[sl, k, pl.ds(r, 1)], sem.at[sl]).start()

    def wait_rows(sl):
        for k in range(TOP_K):
            pltpu.make_async_copy(y_hbm.at[pl.ds(0, tm)], ybuf.at[sl, k], sem.at[sl]).wait()

    @pl.when(i == 0)
    def _():
        start_rows(0, 0)

    start_rows(jnp.minimum(i + 1, n - 1), 1 - slot)
    wait_rows(slot)
    gw = gw_ref[...]
    moe = ybuf[slot, 0] * gw[:, 0:1] + ybuf[slot, 1] * gw[:, 1:2]
    x2 = _layer_norm(alpha * x1_ref[...] + moe, g_ref[...], b_ref[...])
    gate = jax.nn.sigmoid(jnp.dot(x2.astype(BF16), wpg_ref[...], preferred_element_type=F32))
    p = jnp.where(i < n_first, pa_ref[...], pb_ref[...])
    emb = jnp.dot(p.astype(BF16), wpi_ref[...], preferred_element_type=F32)
    res = x2 + gate * emb

    @pl.when(i < n_first)
    def _():
        outa_ref[...] = res

    @pl.when(i >= n_first)
    def _():
        outb_ref[...] = res

    @pl.when(i == n - 1)
    def _():
        wait_rows(1 - slot)


def _combine_ln_gate(y, dest, gw_t, x1, ln_g, ln_b, pa, pb, w_pl_gate, w_pl_in, *, alpha):
    t, d = x1.shape
    tm = min(CMB_TM, pa.shape[0])
    assert t % tm == 0
    n_first = _n_first_tiles(pa, tm)
    row = lambda width: pl.BlockSpec((tm, width), lambda m, ds: (m, 0))
    full = lambda a: pl.BlockSpec(a.shape, lambda m, ds: (0,) * a.ndim)
    grid_spec = pltpu.PrefetchScalarGridSpec(
        num_scalar_prefetch=1,
        grid=(t // tm,),
        in_specs=[
            pl.BlockSpec(memory_space=pl.ANY),
            row(TOP_K), row(d), full(ln_g), full(ln_b), *_pair_specs(tm, pa.shape[1], n_first),
            full(w_pl_gate), full(w_pl_in),
        ],
        out_specs=list(_pair_specs(tm, d, n_first)),
        scratch_shapes=[pltpu.VMEM((2, TOP_K, tm, d), F32), pltpu.SemaphoreType.DMA((2,))],
    )
    return pl.pallas_call(
        functools.partial(_combine_kernel, alpha=alpha, n_tokens=t, n_first=n_first),
        grid_spec=grid_spec,
        out_shape=[jax.ShapeDtypeStruct((pa.shape[0], d), F32),
                   jax.ShapeDtypeStruct((pb.shape[0], d), F32)],
        compiler_params=_params("arbitrary"),
        name="combine_ln_gate",
    )(dest, y, gw_t, x1, ln_g, ln_b, pa, pb, w_pl_gate, w_pl_in)


def kernel(x_prompt, x_sample, p_prompt, p_sample, a_w_qkv, a_w_o, a_sink, b_w_qkv, b_w_o, b_lambda_q1, b_lambda_k1, b_lambda_q2, b_lambda_k2, b_subln_g, ln1_g, ln1_b, ln2_g, ln2_b, router_w, router_bias, exp_w_gate, exp_w_up, exp_w_down, pl_w_in, pl_w_gate):
    depth = ln1_g.shape[0]
    d = x_prompt.shape[-1]
    seq = x_prompt.shape[1]
    assert x_sample.shape[1] == seq
    n_prompt = x_prompt.shape[0]
    bsz = n_prompt + x_sample.shape[0]
    t = bsz * seq
    n_heads = d // HEAD_DIM
    alpha = (2.0 * depth) ** 0.25

    xa = x_prompt.reshape(n_prompt * seq, d)
    xb = x_sample.reshape(t - n_prompt * seq, d)
    pl_dim = p_prompt.shape[-1]
    w_gate, w_up, w_down = (w.astype(BF16) for w in (exp_w_gate, exp_w_up, exp_w_down))

    a_tables = _rope_tables(seq, HEAD_DIM, HEAD_DIM ** -0.5 * math.log2(math.e))
    b_tables = _rope_tables(seq, HEAD_DIM // 2, (HEAD_DIM // 2) ** -0.5 * math.log2(math.e))
    router_wt = router_w.T.astype(BF16)
    router_b = router_bias.astype(F32).reshape(N_EXPERTS, 1)
    row2d = lambda v: v.astype(F32).reshape(1, -1)

    for i in range(depth):
        j = i // N_MIXERS
        if i % N_MIXERS == 0:
            n_qk = (n_heads + A_KV_HEADS) * HEAD_DIM
            qk = _qkv_rope(xa, xb, a_w_qkv[j][:, :n_qk].astype(BF16), a_tables, seq=seq,
                           head_dim=HEAD_DIM, n_q_cols=n_heads * HEAD_DIM,
                           n_k_cols=A_KV_HEADS * HEAD_DIM)
            vt = _vt_proj(xa, xb, a_w_qkv[j][:, n_qk:].T.astype(BF16), seq=seq)
            o = _window_attention(qk, vt, a_sink[j].astype(F32), bsz=bsz, seq=seq,
                                  n_q_heads=n_heads)
            w_o = a_w_o[j]
        else:
            lambda_init = 0.8 - 0.6 * math.exp(-0.3 * i)
            n_qk = 2 * n_heads * HEAD_DIM
            qk = _qkv_rope(xa, xb, b_w_qkv[j][:, :n_qk].astype(BF16), b_tables, seq=seq,
                           head_dim=HEAD_DIM // 2, n_q_cols=n_heads * HEAD_DIM,
                           n_k_cols=n_heads * HEAD_DIM)
            vt = _vt_proj(xa, xb, b_w_qkv[j][:, n_qk:].T.astype(BF16), seq=seq)
            lam_params = jnp.stack([b_lambda_q1[j], b_lambda_k1[j], b_lambda_q2[j],
                                    b_lambda_k2[j]]).astype(F32)
            o = _diff_attention(qk, vt, lam_params, row2d(b_subln_g[j]), bsz=bsz, seq=seq,
                                n_heads=n_heads, lambda_init=lambda_init)
            w_o = b_w_o[j]
        x1, eidx, gw = _oproj_ln_route(o, w_o.astype(BF16), xa, xb, row2d(ln1_g[i]),
                                       row2d(ln1_b[i]), router_wt, router_b, alpha=alpha)
        tok, dest, blk_e, n_used = _dispatch_plan(eidx, MOE_TM)
        y = _moe_experts(x1, tok, blk_e, n_used, w_gate, w_up, w_down, layer=i)
        xa, xb = _combine_ln_gate(y, dest, gw.T, x1, row2d(ln2_g[i]), row2d(ln2_b[i]),
                                  p_prompt[i].reshape(-1, pl_dim), p_sample[i].reshape(-1, pl_dim),
                                  pl_w_gate[i].astype(BF16), pl_w_in[i].astype(BF16), alpha=alpha)

    return (xa.reshape(x_prompt.shape), xb.reshape(x_sample.shape))
```
